```python
import math
import jax, jax.numpy as jnp
from jax import lax
import numpy as np

D_MODEL = 1024
BATCH = 8
SEQ = 8192
DEPTH = 1
DEC_BATCH = 8
DEC_SEQ = 64
PAST_LEN = 1024

CHUNK = 64
Q_BLOCK = 128
D_MIX = 2 * D_MODEL
A_WIDTH = D_MIX // 2
B_WIDTH = D_MIX - A_WIDTH
A_V_DIM = 128
A_HEADS = A_WIDTH // A_V_DIM
A_QK_DIM = A_V_DIM // 2
ROT_DIM = A_QK_DIM // 4
ROPE_THETA = 500000.0
B_HEAD_DIM = 64
B_HEADS = B_WIDTH // B_HEAD_DIM
W_RANK = 64
A_RANK = 64
NORM_EPS = 1e-6
SUBLN_EPS = 1e-5
GN_EPS = 64e-5

A_Q = 0
A_K = A_Q + A_WIDTH
A_V = A_K + A_WIDTH
A_G = A_V + A_WIDTH
B_R = A_G + A_WIDTH
B_K = B_R + B_WIDTH
B_V = B_K + B_WIDTH
B_WD = B_V + B_WIDTH
B_AD = B_WD + W_RANK
B_G = B_AD + A_RANK
IN_TOTAL = B_G + B_WIDTH
SHIFT_W = B_G - B_R

kernel_name = 'hybrid_diffattn_rwkv7_stream_step'

F32 = jnp.float32


def _rms_norm(x, g, eps):
    xf = x.astype(F32)
    y = xf * lax.rsqrt(jnp.mean(xf * xf, axis=-1, keepdims=True) + eps)
    return (y * g.astype(F32)).astype(x.dtype)


def _partial_rope(x, pos):
    half = ROT_DIM // 2
    inv = jnp.power(jnp.float32(ROPE_THETA), -jnp.arange(half, dtype=F32) * (2.0 / ROT_DIM))
    ang = pos.astype(F32)[:, None] * inv[None, :]
    cos = jnp.cos(ang)[None, :, None, None, :]
    sin = jnp.sin(ang)[None, :, None, None, :]
    xf = x.astype(F32)
    x1 = xf[..., :half]
    x2 = xf[..., half:ROT_DIM]
    out = jnp.concatenate([x1 * cos - x2 * sin, x2 * cos + x1 * sin, xf[..., ROT_DIM:]], axis=-1)
    return out.astype(x.dtype)


def _diff_attend(q, k, v, q_pos, k_pos, lam):
    s = jnp.einsum('bqhcd,bkhcd->bchqk', q, k).astype(F32) * (A_QK_DIM ** -0.5)
    visible = (k_pos[None, :] // CHUNK) <= (q_pos[:, None] // CHUNK)
    s = jnp.where(visible[None, None, None], s, -jnp.inf)
    p = jax.nn.softmax(s, axis=-1)
    wgt = p[:, 0] - lam * p[:, 1]
    return jnp.einsum('bhqk,bkhd->bqhd', wgt.astype(v.dtype), v)


def _wkv_scan(r, w, k, v, a, b, s0):
    def step(s, inp):
        r_t, w_t, k_t, v_t, a_t, b_t = inp
        sa = jnp.einsum('bhij,bhj->bhi', s, a_t)
        s = s * w_t[:, :, None, :] + sa[..., None] * b_t[:, :, None, :] + v_t[..., None] * k_t[:, :, None, :]
        return s, jnp.einsum('bhij,bhj->bhi', s, r_t)
    xs = tuple(jnp.moveaxis(t, 1, 0) for t in (r, w, k, v, a, b))
    s_T, ys = lax.scan(step, s0, xs)
    return jnp.moveaxis(ys, 0, 1), s_T


def _layer(x, pos, past_k, past_v, past_pos, wkv0, shift0, lw, lambda_init):
    (norm_pre, w_in, lam_q1, lam_k1, lam_q2, lam_k2, subln, mu_shift, w0, w_up, a0, a_up,
     k_k, k_a, r_k, ln_x_w, ln_x_b, w_out, norm_post) = lw
    bsz, t_len, _ = x.shape
    h = _rms_norm(x, norm_pre, NORM_EPS)
    proj = h @ w_in

    q = _partial_rope(proj[..., A_Q:A_K].reshape(bsz, t_len, A_HEADS, 2, A_QK_DIM), pos)
    k = _partial_rope(proj[..., A_K:A_V].reshape(bsz, t_len, A_HEADS, 2, A_QK_DIM), pos)
    v = proj[..., A_V:A_G].reshape(bsz, t_len, A_HEADS, A_V_DIM)
    lam = (jnp.exp(jnp.sum(lam_q1.astype(F32) * lam_k1.astype(F32)))
           - jnp.exp(jnp.sum(lam_q2.astype(F32) * lam_k2.astype(F32))) + lambda_init)
    if past_k is None:
        nblk = t_len // Q_BLOCK
        qb = jnp.moveaxis(q.reshape(bsz, nblk, Q_BLOCK, A_HEADS, 2, A_QK_DIM), 1, 0)
        pb = pos.reshape(nblk, Q_BLOCK)
        o = lax.map(lambda qp: _diff_attend(qp[0], k, v, qp[1], pos, lam), (qb, pb))
        o = jnp.moveaxis(o, 0, 1).reshape(bsz, t_len, A_HEADS, A_V_DIM)
    else:
        k_all = jnp.concatenate([past_k.astype(k.dtype), k], axis=1)
        v_all = jnp.concatenate([past_v.astype(v.dtype), v], axis=1)
        k_pos = jnp.concatenate([past_pos, pos])
        o = _diff_attend(q, k_all, v_all, pos, k_pos, lam)
    o = _rms_norm(o, subln, SUBLN_EPS).astype(F32) * (1.0 - lambda_init)
    y_a = o.reshape(bsz, t_len, A_WIDTH) * jax.nn.silu(proj[..., A_G:B_R].astype(F32))

    ps = proj[..., B_R:B_G]
    prev = jnp.concatenate([shift0.astype(ps.dtype), ps[:, :-1]], axis=1)
    m = (ps + (prev - ps) * mu_shift).astype(F32)
    r = m[..., 0:B_WIDTH]
    kb = m[..., B_WIDTH:2 * B_WIDTH]
    vb = m[..., 2 * B_WIDTH:3 * B_WIDTH]
    wd = m[..., 3 * B_WIDTH:3 * B_WIDTH + W_RANK]
    ad = m[..., 3 * B_WIDTH + W_RANK:]
    w_log = -jax.nn.softplus(-(w0.astype(F32) + jnp.tanh(wd) @ w_up.astype(F32))) - 0.5
    decay = jnp.exp(-jnp.exp(w_log))
    a = jax.nn.sigmoid(a0.astype(F32) + ad @ a_up.astype(F32))
    hs = lambda t: t.reshape(bsz, t_len, B_HEADS, B_HEAD_DIM)
    kk = hs(kb * k_k.astype(F32))
    kk = kk / jnp.maximum(jnp.sqrt(jnp.sum(kk * kk, axis=-1, keepdims=True)), 1e-12)
    kb = kb * (1.0 + (a - 1.0) * k_a.astype(F32))
    rh, kh, vh, ah = hs(r), hs(kb), hs(vb), hs(a)
    yb, s_T = _wkv_scan(rh, hs(decay), kh, vh, -kk, kk * ah, wkv0.astype(F32))
    mean = jnp.mean(yb, axis=-1, keepdims=True)
    var = jnp.mean(jnp.square(yb - mean), axis=-1, keepdims=True)
    yb = ((yb - mean) * lax.rsqrt(var + GN_EPS)).reshape(bsz, t_len, B_WIDTH)
    yb = yb * ln_x_w.astype(F32) + ln_x_b.astype(F32)
    bonus = jnp.sum(rh * kh * r_k.astype(F32), axis=-1, keepdims=True) * vh
    yb = (yb + bonus.reshape(bsz, t_len, B_WIDTH)) * jax.nn.silu(proj[..., B_G:IN_TOTAL].astype(F32))

    mix = jnp.concatenate([y_a, yb], axis=-1).astype(x.dtype)
    out = mix @ w_out
    x_new = x + _rms_norm(out, norm_post, NORM_EPS)
    return x_new, k, v, s_T.astype(x.dtype), ps[:, -1:]


def setup_inputs(seed: int = 0) -> dict:
    key = jax.random.key(seed)
    ks = jax.random.split(key, 32)
    L = DEPTH
    nrm = lambda k, shape, scale: jax.random.normal(k, shape, F32) * scale
    return {
        'x_prompt': nrm(ks[0], (BATCH, SEQ, D_MODEL), 1.0),
        'x_sample': nrm(ks[1], (DEC_BATCH, DEC_SEQ, D_MODEL), 1.0),
        'cache_k': nrm(ks[2], (L, DEC_BATCH, PAST_LEN, A_HEADS, 2, A_QK_DIM), 1.0),
        'cache_v': nrm(ks[3], (L, DEC_BATCH, PAST_LEN, A_HEADS, A_V_DIM), 1.0),
        'state_wkv': nrm(ks[4], (L, DEC_BATCH, B_HEADS, B_HEAD_DIM, B_HEAD_DIM), 0.3),
        'state_shift': nrm(ks[5], (L, DEC_BATCH, 1, SHIFT_W), 1.0),
        'norm_pre': 1.0 + nrm(ks[6], (L, D_MODEL), 0.05),
        'w_in': nrm(ks[7], (L, D_MODEL, IN_TOTAL), D_MODEL ** -0.5),
        'lam_q1': nrm(ks[8], (L, A_QK_DIM), 0.1),
        'lam_k1': nrm(ks[9], (L, A_QK_DIM), 0.1),
        'lam_q2': nrm(ks[10], (L, A_QK_DIM), 0.1),
        'lam_k2': nrm(ks[11], (L, A_QK_DIM), 0.1),
        'subln': 1.0 + nrm(ks[12], (L, A_V_DIM), 0.05),
        'mu_shift': jax.random.uniform(ks[13], (L, SHIFT_W), F32, 0.0, 1.0),
        'w0': jax.random.uniform(ks[14], (L, B_WIDTH), F32, -2.0, 2.0),
        'w_up': nrm(ks[15], (L, W_RANK, B_WIDTH), 0.1 * W_RANK ** -0.5),
        'a0': nrm(ks[16], (L, B_WIDTH), 0.1),
        'a_up': nrm(ks[17], (L, A_RANK, B_WIDTH), 0.5 * A_RANK ** -0.5),
        'k_k': 0.85 + nrm(ks[18], (L, B_WIDTH), 0.05),
        'k_a': 1.0 + nrm(ks[19], (L, B_WIDTH), 0.05),
        'r_k': nrm(ks[20], (L, B_HEADS, B_HEAD_DIM), 0.1),
        'ln_x_w': 1.0 + nrm(ks[21], (L, B_WIDTH), 0.05),
        'ln_x_b': nrm(ks[22], (L, B_WIDTH), 0.02),
        'w_out': nrm(ks[23], (L, D_MIX, D_MODEL), D_MIX ** -0.5),
        'norm_post': 1.0 + nrm(ks[24], (L, D_MODEL), 0.05),
    }


def reference(x_prompt, x_sample, cache_k, cache_v, state_wkv, state_shift, norm_pre, w_in,
              lam_q1, lam_k1, lam_q2, lam_k2, subln, mu_shift, w0, w_up, a0, a_up, k_k, k_a,
              r_k, ln_x_w, ln_x_b, w_out, norm_post):
    b_p, t_p, _ = x_prompt.shape
    b_s, t_s, _ = x_sample.shape
    past_len = cache_k.shape[2]
    pos_p = jnp.arange(t_p, dtype=jnp.int32)
    pos_s = past_len + jnp.arange(t_s, dtype=jnp.int32)
    past_pos = jnp.arange(past_len, dtype=jnp.int32)
    xp, xs = x_prompt, x_sample
    kp_l, vp_l, wp_l, sp_l = [], [], [], []
    ks_l, vs_l, ws_l, ss_l = [], [], [], []
    for l in range(DEPTH):
        lambda_init = 0.8 - 0.6 * math.exp(-0.3 * l)
        lw = (norm_pre[l], w_in[l], lam_q1[l], lam_k1[l], lam_q2[l], lam_k2[l], subln[l],
              mu_shift[l], w0[l], w_up[l], a0[l], a_up[l], k_k[l], k_a[l], r_k[l],
              ln_x_w[l], ln_x_b[l], w_out[l], norm_post[l])
        wkv_zero = jnp.zeros((b_p, B_HEADS, B_HEAD_DIM, B_HEAD_DIM), F32)
        shift_zero = jnp.zeros((b_p, 1, SHIFT_W), xp.dtype)
        xp, kp, vp, wp, sp = _layer(xp, pos_p, None, None, None, wkv_zero, shift_zero, lw, lambda_init)
        xs, k_s, v_s, w_s, s_s = _layer(xs, pos_s, cache_k[l], cache_v[l], past_pos,
                                       state_wkv[l], state_shift[l], lw, lambda_init)
        kp_l.append(kp); vp_l.append(vp); wp_l.append(wp); sp_l.append(sp)
        ks_l.append(k_s); vs_l.append(v_s); ws_l.append(w_s.astype(state_wkv.dtype)); ss_l.append(s_s)
    return (xp, xs,
            jnp.stack(kp_l), jnp.stack(vp_l), jnp.stack(wp_l), jnp.stack(sp_l),
            jnp.stack(ks_l), jnp.stack(vs_l), jnp.stack(ws_l), jnp.stack(ss_l))
```

```python
import functools
import math

import jax
import jax.numpy as jnp
from jax import lax
from jax.experimental import pallas as pl
from jax.experimental.pallas import tpu as pltpu

F32 = jnp.float32
BF16 = jnp.bfloat16

CHUNK = 64
HEAD_V = 128
HEAD_QK = 64
ROT_DIM = 16
ROPE_THETA = 500000.0
RW_HEAD = 64
RW_CHUNK = 64
LORA = 64
NORM_EPS = 1e-6
SUBLN_EPS = 1e-5
GN_EPS = 64e-5
LANES = 128
MXU_N = 256
VMEM_LIMIT = 56 * 1024 * 1024


def _sigmoid(x):
    return 1.0 / (1.0 + jnp.exp(-x))


def _softplus(x):
    return jnp.maximum(x, 0.0) + jnp.log1p(jnp.exp(-jnp.abs(x)))


def _dot(a, b):
    return jnp.dot(a, b, preferred_element_type=F32)


def _dot_nt(a, b):
    return lax.dot_general(a, b, (((1,), (1,)), ((), ())), preferred_element_type=F32)


def _dot_tn(a, b):
    return lax.dot_general(a, b, (((0,), (0,)), ((), ())), preferred_element_type=F32)


def _inproj_body(x_ref, shift0_ref, npre_ref, mu_ref, cos_ref, s1_ref, s2_ref, w_ref,
                 q_ref, k_ref, v_ref, ga_ref, m_ref, gb_ref, shift_ref, h_scr, carry_scr,
                 *, tiles_per_batch, tm, width, shift_w, col_chunk):
    i = pl.program_id(0)

    @pl.when(i == 0)
    def _():
        carry_scr[...] = jnp.zeros_like(carry_scr)

    x = x_ref[...]
    y = x * lax.rsqrt(jnp.mean(x * x, axis=-1, keepdims=True) + NORM_EPS)
    h_scr[...] = (y * npre_ref[...]).astype(BF16)

    def proj(c0, cw):
        return _dot(h_scr[...], w_ref[:, c0:c0 + cw])

    cos = cos_ref[...]
    s1 = s1_ref[...]
    s2 = s2_ref[...]

    def rope(p):
        outs = []
        for g in range(p.shape[1] // LANES):
            xg = p[:, g * LANES:(g + 1) * LANES]
            outs.append(xg * cos + pltpu.roll(xg, ROT_DIM // 2, axis=1) * s1
                        + pltpu.roll(xg, LANES - ROT_DIM // 2, axis=1) * s2)
        return jnp.concatenate(outs, axis=1)

    n_sub = width // col_chunk
    a_q, a_k, a_v, a_g = 0, width, 2 * width, 3 * width
    b_s = 4 * width
    b_g = b_s + shift_w
    for c in range(n_sub):
        o = c * col_chunk
        sl = slice(o, o + col_chunk)
        q_ref[:, sl] = (rope(proj(a_q + o, col_chunk)) * (HEAD_QK ** -0.5)).astype(BF16)
        k_ref[:, sl] = rope(proj(a_k + o, col_chunk))
        v_ref[:, sl] = proj(a_v + o, col_chunk)
        g = proj(a_g + o, col_chunk)
        ga_ref[:, sl] = (g * _sigmoid(g)).astype(BF16)
        g = proj(b_g + o, col_chunk)
        gb_ref[:, sl] = (g * _sigmoid(g)).astype(BF16)

    is_first = (i % tiles_per_batch) == 0
    o = 0
    while o < shift_w:
        cw = min(col_chunk, shift_w - o)
        sl = slice(o, o + cw)
        ps = proj(b_s + o, cw)
        prev_row = jnp.where(is_first, shift0_ref[0, :, sl], carry_scr[:, sl])
        rows = lax.broadcasted_iota(jnp.int32, ps.shape, 0)
        prev = jnp.where(rows == 0, prev_row, pltpu.roll(ps, 1, axis=0))
        last = ps[tm - 1:tm, :]
        carry_scr[:, sl] = last
        shift_ref[0, :, sl] = last
        m_ref[:, sl] = (ps + (prev - ps) * mu_ref[:, sl]).astype(BF16)
        o += cw


def _inproj(x2, shift0, npre, mu, cos, s1, s2, w_bf, *, batch, seq, tm):
    rows, d = x2.shape
    width = (w_bf.shape[1] - 2 * LORA) // 8
    shift_w = 3 * width + 2 * LORA
    tpb = seq // tm
    body = functools.partial(_inproj_body, tiles_per_batch=tpb, tm=tm, width=width,
                             shift_w=shift_w, col_chunk=512)
    row_blk = lambda w: pl.BlockSpec((tm, w), lambda i: (i, 0))
    full = lambda a: pl.BlockSpec(a.shape, lambda i: (0,) * a.ndim)
    tab = pl.BlockSpec((tm, LANES), lambda i: (i % tpb, 0))
    per_b = pl.BlockSpec((1, 1, shift_w), lambda i: (i // tpb, 0, 0))
    return pl.pallas_call(
        body,
        grid=(rows // tm,),
        in_specs=[row_blk(d), per_b, full(npre), full(mu), tab, tab, tab,
                  pl.BlockSpec(memory_space=pltpu.VMEM)],
        out_specs=[row_blk(width), row_blk(width), row_blk(width), row_blk(width),
                   row_blk(shift_w), row_blk(width), per_b],
        out_shape=[jax.ShapeDtypeStruct((rows, width), BF16),
                   jax.ShapeDtypeStruct((rows, width), F32),
                   jax.ShapeDtypeStruct((rows, width), F32),
                   jax.ShapeDtypeStruct((rows, width), BF16),
                   jax.ShapeDtypeStruct((rows, shift_w), BF16),
                   jax.ShapeDtypeStruct((rows, width), BF16),
                   jax.ShapeDtypeStruct((batch, 1, shift_w), F32)],
        scratch_shapes=[pltpu.VMEM((tm, d), BF16), pltpu.VMEM((1, shift_w), F32)],
        compiler_params=pltpu.CompilerParams(dimension_semantics=("arbitrary",),
                                             vmem_limit_bytes=VMEM_LIMIT),
        name="inproj",
    )(x2, shift0, npre, mu, cos, s1, s2, w_bf)


def _attn_body(lam_ref, subln_ref, q_ref, k_ref, v_ref, ga_ref, ya_ref,
               kb_scr, vb_scr, m_scr, l_scr, acc_scr, *, tq, tk, q_off, lambda_init):
    qi = pl.program_id(2)
    n_kv = kb_scr.shape[0] // tk

    @pl.when(qi == 0)
    def _():
        def cvt(c, carry):
            sl = pl.ds(pl.multiple_of(c * tk, tk), tk)
            kb_scr[sl, :] = k_ref[0, sl, :].astype(BF16)
            vb_scr[sl, :] = v_ref[0, sl, :].astype(BF16)
            return carry
        lax.fori_loop(0, n_kv, cvt, 0)

    q = q_ref[...]
    lane = lax.broadcasted_iota(jnp.int32, q.shape, 1)
    zero = jnp.zeros_like(q)
    qs = jnp.concatenate([jnp.where(lane < HEAD_QK, q, zero),
                          jnp.where(lane >= HEAD_QK, q, zero)], axis=0)

    m_scr[...] = jnp.full_like(m_scr, -jnp.inf)
    l_scr[...] = jnp.zeros_like(l_scr)
    acc_scr[...] = jnp.zeros_like(acc_scr)

    def step(j, masked):
        sl = pl.ds(pl.multiple_of(j * tk, tk), tk)
        s = _dot_nt(qs, kb_scr[sl, :])
        if masked:
            qpos = q_off + qi * tq + lax.broadcasted_iota(jnp.int32, s.shape, 0) % tq
            kpos = j * tk + lax.broadcasted_iota(jnp.int32, s.shape, 1)
            s = jnp.where(kpos // CHUNK <= qpos // CHUNK, s, -jnp.inf)
        m_prev = m_scr[...]
        m_new = jnp.maximum(m_prev, jnp.max(s, axis=1, keepdims=True))
        alpha = jnp.exp(m_prev - m_new)
        p = jnp.exp(s - m_new)
        l_scr[...] = alpha * l_scr[...] + jnp.sum(p, axis=1, keepdims=True)
        acc_scr[...] = alpha * acc_scr[...] + _dot(p.astype(BF16), vb_scr[sl, :])
        m_scr[...] = m_new

    def full_step(j, carry):
        step(j, False)
        return carry

    lax.fori_loop(0, qi, full_step, 0)
    step(qi, True)

    lam4 = lam_ref[...]
    e1 = jnp.exp(jnp.sum(lam4[0:1] * lam4[1:2], axis=1, keepdims=True))
    e2 = jnp.exp(jnp.sum(lam4[2:3] * lam4[3:4], axis=1, keepdims=True))
    lam = e1 - e2 + lambda_init
    acc = acc_scr[...]
    l = l_scr[...]
    o = acc[:tq] / l[:tq] - lam * (acc[tq:] / l[tq:])
    y = o * lax.rsqrt(jnp.mean(o * o, axis=-1, keepdims=True) + SUBLN_EPS)
    y = (y * subln_ref[...]) * (1.0 - lambda_init)
    ya_ref[...] = (y * ga_ref[...].astype(F32)).astype(BF16)


def _attention(lam4, subln, q2, k3, v3, ga2, *, batch, tq_len, tq, tk, q_off, lambda_init):
    tk_len = k3.shape[1]
    heads = k3.shape[2] // HEAD_V
    nq = tq_len // tq
    assert nq == 1 or (q_off == 0 and tq == tk and tq_len == tk_len)
    assert nq > 1 or tk == tk_len
    body = functools.partial(_attn_body, tq=tq, tk=tk, q_off=q_off, lambda_init=lambda_init)
    qblk = pl.BlockSpec((tq, HEAD_V), lambda b, h, i: (b * nq + i, h))
    kvblk = pl.BlockSpec((1, tk_len, HEAD_V), lambda b, h, i: (b, 0, h))
    full = lambda a: pl.BlockSpec(a.shape, lambda b, h, i: (0,) * a.ndim)
    return pl.pallas_call(
        body,
        grid=(batch, heads, nq),
        in_specs=[full(lam4), full(subln), qblk, kvblk, kvblk, qblk],
        out_specs=qblk,
        out_shape=jax.ShapeDtypeStruct(q2.shape, BF16),
        scratch_shapes=[pltpu.VMEM((tk_len, HEAD_V), BF16), pltpu.VMEM((tk_len, HEAD_V), BF16),
                        pltpu.VMEM((2 * tq, 1), F32), pltpu.VMEM((2 * tq, 1), F32),
                        pltpu.VMEM((2 * tq, HEAD_V), F32)],
        compiler_params=pltpu.CompilerParams(
            dimension_semantics=("arbitrary", "arbitrary", "arbitrary"),
            vmem_limit_bytes=VMEM_LIMIT),
        name="diffattn",
    )(lam4, subln, q2, k3, v3, ga2)


def _rwkv_body(m_ref, gb_ref, s0_ref, wcomb_ref, w0_ref, a0_ref, kk_ref, ka_ref, rk_ref,
               lnw_ref, lnb_ref, yb_ref, sout_ref, s_scr, *, width):
    c = pl.program_id(1)
    L = RW_CHUNK
    n_pairs = width // LANES

    @pl.when(c == 0)
    def _():
        s_scr[...] = s0_ref[0]

    r = m_ref[:, 0:width].astype(F32)
    kb = m_ref[:, width:2 * width].astype(F32)
    vb = m_ref[:, 2 * width:3 * width].astype(F32)
    z = m_ref[:, 3 * width:3 * width + 2 * LORA].astype(F32)
    lane = lax.broadcasted_iota(jnp.int32, (L, LANES), 1)
    lo_half = lane < RW_HEAD
    zt = jnp.where(lo_half, jnp.tanh(z), z).astype(BF16)
    lin = _dot(zt, wcomb_ref[...])
    w_log = -_softplus(-(w0_ref[...] + lin[:, :width])) - 0.5
    logw = -jnp.exp(w_log)
    alpha = _sigmoid(a0_ref[...] + lin[:, width:])

    gi = lax.broadcasted_iota(jnp.int32, (MXU_N, MXU_N), 0) // RW_HEAD
    gj = lax.broadcasted_iota(jnp.int32, (MXU_N, MXU_N), 1) // RW_HEAD
    group_ones = (gi == gj).astype(BF16)

    def head_sum(t):
        tb = t.astype(BF16)
        return jnp.concatenate([_dot(tb[:, o:o + MXU_N], group_ones)
                                for o in range(0, width, MXU_N)], axis=1)

    kk = kb * kk_ref[...]
    kk = kk / jnp.maximum(jnp.sqrt(head_sum(kk * kk)), 1e-12)
    kb = kb * (1.0 + (alpha - 1.0) * ka_ref[...])

    ti = lax.broadcasted_iota(jnp.int32, (L, L), 0)
    tj = lax.broadcasted_iota(jnp.int32, (L, L), 1)
    tri = (ti >= tj).astype(BF16)
    lw_hi = logw.astype(BF16)
    lw_lo = (logw - lw_hi.astype(F32)).astype(BF16)
    lp = _dot(tri, lw_hi) + _dot(tri, lw_lo)
    p_inv = jnp.exp(-lp)
    r_hat = r * jnp.exp(lp)
    a_hat = -kk * jnp.exp(lp - logw)
    b_til = kk * alpha * p_inv
    k_til = kb * p_inv
    p_last = jnp.exp(lp[L - 1:L, :])

    lane_hi = jnp.logical_not(lo_half)

    def stack(t):
        return jnp.concatenate([jnp.where(lo_half, t, 0.0), jnp.where(lane_hi, t, 0.0)],
                               axis=0).astype(BF16)

    ri = lax.broadcasted_iota(jnp.int32, (2 * L, 2 * L), 0) % L
    ci = lax.broadcasted_iota(jnp.int32, (2 * L, 2 * L), 1) % L
    strict = ri > ci
    incl = ri >= ci
    eye = (lax.broadcasted_iota(jnp.int32, (2 * L, 2 * L), 0)
           == lax.broadcasted_iota(jnp.int32, (2 * L, 2 * L), 1)).astype(F32)

    ys = []
    for g in range(n_pairs):
        sl = slice(g * LANES, (g + 1) * LANES)
        a_s, r_s = stack(a_hat[:, sl]), stack(r_hat[:, sl])
        b_s, k_s, v_s = stack(b_til[:, sl]), stack(k_til[:, sl]), stack(vb[:, sl])
        gram = _dot_nt(jnp.concatenate([a_s, r_s], axis=0), jnp.concatenate([b_s, k_s], axis=0))
        a_ab = jnp.where(strict, gram[:2 * L, :2 * L], 0.0)
        a_ak = jnp.where(strict, gram[:2 * L, 2 * L:], 0.0).astype(BF16)
        a_rb = jnp.where(incl, gram[2 * L:, :2 * L], 0.0).astype(BF16)
        a_rk = jnp.where(incl, gram[2 * L:, 2 * L:], 0.0).astype(BF16)
        t_inv = eye + a_ab
        pw = a_ab
        for _ in range(int(math.log2(L)) - 1):
            pwb = pw.astype(BF16)
            pw = _dot(pwb, pwb)
            t_inv = t_inv + _dot(t_inv.astype(BF16), pw.astype(BF16))
        s_old = s_scr[g]
        s_b = s_old.astype(BF16)
        x = _dot_nt(a_s, s_b) + _dot(a_ak, v_s)
        u = _dot(t_inv.astype(BF16), x.astype(BF16)).astype(BF16)
        y2 = _dot_nt(r_s, s_b) + _dot(a_rb, u) + _dot(a_rk, v_s)
        ys.append(y2[:L] + y2[L:])
        s_scr[g] = (s_old + _dot_tn(u, b_s) + _dot_tn(v_s, k_s)) * p_last[:, sl]
    yb = jnp.concatenate(ys, axis=1)

    inv_n = 1.0 / RW_HEAD
    mean = head_sum(yb) * inv_n
    d = yb - mean
    var = head_sum(d * d) * inv_n
    yn = d * lax.rsqrt(var + GN_EPS) * lnw_ref[...] + lnb_ref[...]
    bonus = head_sum(r * kb * rk_ref[...]) * vb
    yb_ref[...] = ((yn + bonus) * gb_ref[...].astype(F32)).astype(BF16)

    @pl.when(c == pl.num_programs(1) - 1)
    def _():
        sout_ref[0] = s_scr[...]


def _rwkv(m2, gb2, s0p, wcomb, w0, a0, k_k, k_a, r_k, ln_w, ln_b, *, batch, seq):
    rows, shift_w = m2.shape
    width = gb2.shape[1]
    nc = seq // RW_CHUNK
    n_pairs = width // LANES
    body = functools.partial(_rwkv_body, width=width)
    row_blk = lambda w: pl.BlockSpec((RW_CHUNK, w), lambda b, c: (b * nc + c, 0))
    full = lambda a: pl.BlockSpec(a.shape, lambda b, c: (0,) * a.ndim)
    st = pl.BlockSpec((1, n_pairs, LANES, LANES), lambda b, c: (b, 0, 0, 0))
    return pl.pallas_call(
        body,
        grid=(batch, nc),
        in_specs=[row_blk(shift_w), row_blk(width), st, full(wcomb), full(w0), full(a0),
                  full(k_k), full(k_a), full(r_k), full(ln_w), full(ln_b)],
        out_specs=[row_blk(width), st],
        out_shape=[jax.ShapeDtypeStruct((rows, width), BF16),
                   jax.ShapeDtypeStruct(s0p.shape, F32)],
        scratch_shapes=[pltpu.VMEM((n_pairs, LANES, LANES), F32)],
        compiler_params=pltpu.CompilerParams(dimension_semantics=("arbitrary", "arbitrary"),
                                             vmem_limit_bytes=VMEM_LIMIT),
        name="rwkv7",
    )(m2, gb2, s0p, wcomb, w0, a0, k_k, k_a, r_k, ln_w, ln_b)


def _outproj_body(x_ref, ya_ref, yb_ref, wa_ref, wb_ref, npost_ref, y_ref):
    out = _dot(ya_ref[...], wa_ref[...]) + _dot(yb_ref[...], wb_ref[...])
    y = out * lax.rsqrt(jnp.mean(out * out, axis=-1, keepdims=True) + NORM_EPS)
    y_ref[...] = x_ref[...] + y * npost_ref[...]


def _outproj(x2, ya2, yb2, wa, wb, npost, *, tm):
    rows, d = x2.shape
    row_blk = lambda w: pl.BlockSpec((tm, w), lambda i: (i, 0))
    full = lambda a: pl.BlockSpec(a.shape, lambda i: (0,) * a.ndim)
    return pl.pallas_call(
        _outproj_body,
        grid=(rows // tm,),
        in_specs=[row_blk(d), row_blk(ya2.shape[1]), row_blk(yb2.shape[1]),
                  full(wa), full(wb), full(npost)],
        out_specs=row_blk(d),
        out_shape=jax.ShapeDtypeStruct((rows, d), F32),
        compiler_params=pltpu.CompilerParams(dimension_semantics=("arbitrary",),
                                             vmem_limit_bytes=VMEM_LIMIT),
        name="outproj",
    )(x2, ya2, yb2, wa, wb, npost)


def _rope_tables(pos):
    half = ROT_DIM // 2
    inv = jnp.power(jnp.float32(ROPE_THETA), -jnp.arange(half, dtype=F32) * (2.0 / ROT_DIM))
    ang = pos.astype(F32)[:, None] * inv[None, :]
    cos, sin = jnp.cos(ang), jnp.sin(ang)
    n = pos.shape[0]
    pad = HEAD_QK - ROT_DIM
    zeros, ones = jnp.zeros((n, half), F32), jnp.ones((n, pad), F32)
    zpad = jnp.zeros((n, pad), F32)
    c = jnp.concatenate([cos, cos, ones], axis=1)
    s_prev = jnp.concatenate([zeros, sin, zpad], axis=1)
    s_next = jnp.concatenate([-sin, zeros, zpad], axis=1)
    rep = LANES // HEAD_QK
    return tuple(jnp.tile(t, (1, rep)) for t in (c, s_prev, s_next))


def _pair_states(s):
    b, h, n, _ = s.shape
    s = s.reshape(b, h // 2, 2, n, n)
    z = jnp.zeros((b, h // 2, n, n), s.dtype)
    top = jnp.concatenate([s[:, :, 0], z], axis=-1)
    bot = jnp.concatenate([z, s[:, :, 1]], axis=-1)
    return jnp.concatenate([top, bot], axis=-2)


def _unpair_states(sp):
    n = RW_HEAD
    b, g = sp.shape[:2]
    return jnp.stack([sp[:, :, :n, :n], sp[:, :, n:, n:]], axis=2).reshape(b, 2 * g, n, n)


def _stream(x, pos, past_k, past_v, wkv0, shift0, p, lambda_init):
    batch, seq, d = x.shape
    width = p["w_out_a"].shape[0]
    heads = width // HEAD_V
    rows = batch * seq
    x2 = x.reshape(rows, d)
    tm = min(256, seq)
    cos, s1, s2 = _rope_tables(pos)
    q2, k2, v2, ga2, m2, gb2, shift_out = _inproj(
        x2, shift0, p["norm_pre"], p["mu_shift"], cos, s1, s2, p["w_in"],
        batch=batch, seq=seq, tm=tm)

    k3 = k2.reshape(batch, seq, width)
    v3 = v2.reshape(batch, seq, width)
    if past_k is None:
        k_all, v_all, q_off = k3, v3, 0
        tq = tk = min(256, seq)
    else:
        past = past_k.shape[1]
        total = past + seq
        padded = -(-total // LANES) * LANES
        zpad = jnp.zeros((batch, padded - total, width), F32)
        k_all = jnp.concatenate([past_k.reshape(batch, past, width), k3, zpad], axis=1)
        v_all = jnp.concatenate([past_v.reshape(batch, past, width), v3, zpad], axis=1)
        q_off, tq, tk = past, seq, padded
    ya2 = _attention(p["lam4"], p["subln"], q2, k_all, v_all, ga2, batch=batch, tq_len=seq,
                     tq=tq, tk=tk, q_off=q_off, lambda_init=lambda_init)

    yb2, s_pairs = _rwkv(m2, gb2, _pair_states(wkv0), p["w_comb"], p["w0"], p["a0"], p["k_k"],
                         p["k_a"], p["r_k"], p["ln_x_w"], p["ln_x_b"], batch=batch, seq=seq)

    y2 = _outproj(x2, ya2, yb2, p["w_out_a"], p["w_out_b"], p["norm_post"], tm=tm)
    return (y2.reshape(batch, seq, d),
            k3.reshape(batch, seq, heads, 2, HEAD_QK),
            v3.reshape(batch, seq, heads, HEAD_V),
            _unpair_states(s_pairs),
            shift_out)


def kernel(x_prompt, x_sample, cache_k, cache_v, state_wkv, state_shift, norm_pre, w_in, lam_q1, lam_k1, lam_q2, lam_k2, subln, mu_shift, w0, w_up, a0, a_up, k_k, k_a, r_k, ln_x_w, ln_x_b, w_out, norm_post):
    depth = w_in.shape[0]
    assert depth == 1, "single-layer problem"
    l = 0
    lambda_init = 0.8 - 0.6 * math.exp(-0.3 * l)
    width = w_up.shape[2]
    row = lambda a: a[l].reshape(1, -1).astype(F32)
    zeros = jnp.zeros((LORA, width), F32)
    p = {
        "norm_pre": row(norm_pre), "norm_post": row(norm_post), "subln": row(subln),
        "mu_shift": row(mu_shift), "w0": row(w0), "a0": row(a0), "k_k": row(k_k), "k_a": row(k_a),
        "r_k": row(r_k), "ln_x_w": row(ln_x_w), "ln_x_b": row(ln_x_b),
        "w_in": w_in[l].astype(BF16),
        "w_out_a": w_out[l, :width].astype(BF16), "w_out_b": w_out[l, width:].astype(BF16),
        "w_comb": jnp.concatenate([jnp.concatenate([w_up[l], zeros], axis=1),
                                   jnp.concatenate([zeros, a_up[l]], axis=1)], axis=0).astype(BF16),
        "lam4": jnp.stack([lam_q1[l], lam_k1[l], lam_q2[l], lam_k2[l]]).astype(F32),
    }
    b_p, t_p, _ = x_prompt.shape
    b_s, t_s, _ = x_sample.shape
    past_len = cache_k.shape[2]
    shift_w = state_shift.shape[-1]
    n_heads_b = state_wkv.shape[2]
    pos_p = jnp.arange(t_p, dtype=jnp.int32)
    pos_s = past_len + jnp.arange(t_s, dtype=jnp.int32)

    yp, kp, vp, wp, sp = _stream(
        x_prompt, pos_p, None, None,
        jnp.zeros((b_p, n_heads_b, RW_HEAD, RW_HEAD), F32), jnp.zeros((b_p, 1, shift_w), F32),
        p, lambda_init)
    ys, ks, vs, ws, ss = _stream(
        x_sample, pos_s, cache_k[l], cache_v[l], state_wkv[l].astype(F32), state_shift[l],
        p, lambda_init)
    return (yp, ys, kp[None], vp[None], wp[None], sp[None],
            ks[None], vs[None], ws[None].astype(state_wkv.dtype), ss[None])
```

```python
import functools
import math

import jax
import jax.numpy as jnp
from jax import lax
from jax.experimental import pallas as pl
from jax.experimental.pallas import tpu as pltpu

F32 = jnp.float32
BF16 = jnp.bfloat16

CHUNK = 64
HEAD_V = 128
HEAD_QK = 64
ROT_DIM = 16
ROPE_THETA = 500000.0
RW_HEAD = 64
RW_CHUNK = 64
LORA = 64
NORM_EPS = 1e-6
SUBLN_EPS = 1e-5
GN_EPS = 64e-5
LANES = 128
MXU_N = 256
VMEM_LIMIT = 56 * 1024 * 1024
Q_SCALE = HEAD_QK ** -0.5 * math.log2(math.e)


def _sigmoid(x):
    return 1.0 / (1.0 + jnp.exp(-x))


def _softplus(x):
    return jnp.maximum(x, 0.0) + jnp.log1p(jnp.exp(-jnp.abs(x)))


def _dot(a, b):
    return jnp.dot(a, b, preferred_element_type=F32)


def _dot_nt(a, b):
    return lax.dot_general(a, b, (((1,), (1,)), ((), ())), preferred_element_type=F32)


def _dot_tn(a, b):
    return lax.dot_general(a, b, (((0,), (0,)), ((), ())), preferred_element_type=F32)


def _inproj_body(x_ref, shift0_ref, npre_ref, mu_ref, cos_ref, s1_ref, s2_ref, w_ref,
                 q_ref, k_ref, v_ref, ga_ref, m_ref, gb_ref, shift_ref, h_scr, carry_scr,
                 *, tiles_per_batch, tm, width, shift_w, col_chunk):
    i = pl.program_id(0)

    @pl.when(i == 0)
    def _():
        carry_scr[...] = jnp.zeros_like(carry_scr)

    x = x_ref[...]
    y = x * lax.rsqrt(jnp.mean(x * x, axis=-1, keepdims=True) + NORM_EPS)
    h_scr[...] = (y * npre_ref[...]).astype(BF16)

    def proj(c0, cw):
        return _dot(h_scr[...], w_ref[:, c0:c0 + cw])

    cos = cos_ref[...]
    s1 = s1_ref[...]
    s2 = s2_ref[...]

    def rope(p):
        outs = []
        for g in range(p.shape[1] // LANES):
            xg = p[:, g * LANES:(g + 1) * LANES]
            outs.append(xg * cos + pltpu.roll(xg, ROT_DIM // 2, axis=1) * s1
                        + pltpu.roll(xg, LANES - ROT_DIM // 2, axis=1) * s2)
        return jnp.concatenate(outs, axis=1)

    n_sub = width // col_chunk
    a_q, a_k, a_v, a_g = 0, width, 2 * width, 3 * width
    b_s = 4 * width
    b_g = b_s + shift_w
    for c in range(n_sub):
        o = c * col_chunk
        sl = slice(o, o + col_chunk)
        q_ref[:, sl] = (rope(proj(a_q + o, col_chunk)) * Q_SCALE).astype(BF16)
        k_ref[:, sl] = rope(proj(a_k + o, col_chunk))
        v_ref[:, sl] = proj(a_v + o, col_chunk)
        g = proj(a_g + o, col_chunk)
        ga_ref[:, sl] = (g * _sigmoid(g)).astype(BF16)
        g = proj(b_g + o, col_chunk)
        gb_ref[:, sl] = (g * _sigmoid(g)).astype(BF16)

    is_first = (i % tiles_per_batch) == 0
    o = 0
    while o < shift_w:
        cw = min(col_chunk, shift_w - o)
        sl = slice(o, o + cw)
        ps = proj(b_s + o, cw)
        prev_row = jnp.where(is_first, shift0_ref[0, :, sl], carry_scr[:, sl])
        rows = lax.broadcasted_iota(jnp.int32, ps.shape, 0)
        prev = jnp.where(rows == 0, prev_row, pltpu.roll(ps, 1, axis=0))
        last = ps[tm - 1:tm, :]
        carry_scr[:, sl] = last
        shift_ref[0, :, sl] = last
        m_ref[:, sl] = (ps + (prev - ps) * mu_ref[:, sl]).astype(BF16)
        o += cw


def _inproj(x2, shift0, npre, mu, cos, s1, s2, w_bf, *, batch, seq, tm):
    rows, d = x2.shape
    width = (w_bf.shape[1] - 2 * LORA) // 8
    shift_w = 3 * width + 2 * LORA
    tpb = seq // tm
    body = functools.partial(_inproj_body, tiles_per_batch=tpb, tm=tm, width=width,
                             shift_w=shift_w, col_chunk=512)
    row_blk = lambda w: pl.BlockSpec((tm, w), lambda i: (i, 0))
    full = lambda a: pl.BlockSpec(a.shape, lambda i: (0,) * a.ndim)
    tab = pl.BlockSpec((tm, LANES), lambda i: (i % tpb, 0))
    per_b = pl.BlockSpec((1, 1, shift_w), lambda i: (i // tpb, 0, 0))
    return pl.pallas_call(
        body,
        grid=(rows // tm,),
        in_specs=[row_blk(d), per_b, full(npre), full(mu), tab, tab, tab,
                  pl.BlockSpec(memory_space=pltpu.VMEM)],
        out_specs=[row_blk(width), row_blk(width), row_blk(width), row_blk(width),
                   row_blk(shift_w), row_blk(width), per_b],
        out_shape=[jax.ShapeDtypeStruct((rows, width), BF16),
                   jax.ShapeDtypeStruct((rows, width), F32),
                   jax.ShapeDtypeStruct((rows, width), F32),
                   jax.ShapeDtypeStruct((rows, width), BF16),
                   jax.ShapeDtypeStruct((rows, shift_w), BF16),
                   jax.ShapeDtypeStruct((rows, width), BF16),
                   jax.ShapeDtypeStruct((batch, 1, shift_w), F32)],
        scratch_shapes=[pltpu.VMEM((tm, d), BF16), pltpu.VMEM((1, shift_w), F32)],
        compiler_params=pltpu.CompilerParams(dimension_semantics=("arbitrary",),
                                             vmem_limit_bytes=VMEM_LIMIT),
        name="inproj",
    )(x2, shift0, npre, mu, cos, s1, s2, w_bf)


def _attn_body(lam_ref, subln_ref, q_ref, k_ref, v_ref, ga_ref, ya_ref,
               kb_scr, vt_scr, qt_scr, sa_scr, sb_scr, cma_scr, cmb_scr, m_scr, acc_scr,
               *, tq, tk, qb, q_off, lambda_init):
    qi = pl.program_id(2)
    n_kv = kb_scr.shape[0] // tk
    n_blk = 2 * tq // qb
    n_diag = max(tq // tk, 1)
    v_rows = vt_scr.shape[1]

    @pl.when(qi == 0)
    def _():
        def cvt(c, carry):
            sl = pl.ds(pl.multiple_of(c * tk, tk), tk)
            kb_scr[sl, :] = k_ref[0, sl, :].astype(BF16)
            vt_scr[c, 0:HEAD_V, :] = v_ref[0, sl, :].T.astype(BF16)
            vt_scr[c, HEAD_V:, :] = jnp.ones((v_rows - HEAD_V, tk), BF16)
            return carry
        lax.fori_loop(0, n_kv, cvt, 0)

    q = q_ref[...].astype(F32)
    lane = lax.broadcasted_iota(jnp.int32, q.shape, 1)
    qs = jnp.concatenate([jnp.where(lane < HEAD_QK, q, 0.0),
                          jnp.where(lane >= HEAD_QK, q, 0.0)], axis=0)
    for c in range(n_blk):
        qt_scr[c] = qs[c * qb:(c + 1) * qb].T.astype(BF16)

    m_scr[...] = jnp.full_like(m_scr, -jnp.inf)
    acc_scr[...] = jnp.zeros_like(acc_scr)

    def diag_kind(t, c):
        q_lo = (c * qb) % tq
        if qb <= tq and q_lo + qb <= t * tk:
            return "skip"
        if qb <= tq and q_off == 0 and q_lo >= (t + 1) * tk:
            return None
        qrel = (c * qb + lax.broadcasted_iota(jnp.int32, (tk, qb), 1)) % tq
        krel = t * tk + lax.broadcasted_iota(jnp.int32, (tk, qb), 0)
        if q_off:
            return krel // CHUNK <= (q_off + qrel) // CHUNK
        return krel // CHUNK <= qrel // CHUNK

    bufs = ((sa_scr, cma_scr), (sb_scr, cmb_scr))

    def scores(j, buf, kinds):
        s_ref, cm_ref = bufs[buf]
        k_j = kb_scr[pl.ds(pl.multiple_of(j * tk, tk), tk), :]
        for c in range(n_blk):
            if isinstance(kinds[c], str):
                continue
            s = _dot(k_j, qt_scr[c])
            if kinds[c] is not None:
                s = jnp.where(kinds[c], s, -jnp.inf)
            s_ref[c] = s
            cm_ref[c] = jnp.max(s, axis=0, keepdims=True)

    def consume(j, buf, kinds):
        s_ref, cm_ref = bufs[buf]
        vt_j = vt_scr[j]
        for c in range(n_blk):
            if isinstance(kinds[c], str):
                continue
            m_prev = m_scr[c]
            m_new = jnp.maximum(m_prev, cm_ref[c])
            alpha = jnp.exp2(m_prev - m_new)
            p = jnp.exp2(s_ref[c] - m_new).astype(BF16)
            acc_scr[c] = alpha * acc_scr[c] + _dot(vt_j, p)
            m_scr[c] = m_new

    all_visible = [None] * n_blk
    diag = [[diag_kind(t, c) for c in range(n_blk)] for t in range(n_diag)]
    n_full = (qi * tq) // tk

    @pl.when(n_full > 0)
    def _():
        scores(0, 0, all_visible)

    @pl.when(n_full == 0)
    def _():
        scores(0, 0, diag[0])

    def tile_pair(j, next_kinds):
        scores(j + 1, 1, all_visible)
        consume(j, 0, all_visible)
        scores(j + 2, 0, next_kinds)
        consume(j + 1, 1, all_visible)

    def pair_step(i, carry):
        tile_pair(2 * i, all_visible)
        return carry

    lax.fori_loop(0, n_full // 2 - 1, pair_step, 0)

    @pl.when(n_full > 0)
    def _():
        tile_pair(n_full - 2, diag[0])

    for t in range(n_diag):
        if t + 1 < n_diag:
            scores(n_full + t + 1, (t + 1) % 2, diag[t + 1])
        consume(n_full + t, t % 2, diag[t])

    lam4 = lam_ref[...]
    e1 = jnp.exp(jnp.sum(lam4[0:1] * lam4[1:2], axis=1, keepdims=True))
    e2 = jnp.exp(jnp.sum(lam4[2:3] * lam4[3:4], axis=1, keepdims=True))
    lam = e1 - e2 + lambda_init
    acc = jnp.concatenate([acc_scr[c] for c in range(n_blk)], axis=1)
    o_all = acc[0:HEAD_V] / acc[HEAD_V:HEAD_V + 1]
    o = (o_all[:, :tq] - lam * o_all[:, tq:]).T
    y = o * lax.rsqrt(jnp.mean(o * o, axis=-1, keepdims=True) + SUBLN_EPS)
    y = (y * subln_ref[...]) * (1.0 - lambda_init)
    ya_ref[...] = (y * ga_ref[...].astype(F32)).astype(BF16)


def _attention(lam4, subln, q2, k3, v3, ga2, *, batch, tq_len, tq, tk, q_off, lambda_init):
    tk_len = k3.shape[1]
    heads = k3.shape[2] // HEAD_V
    nq = tq_len // tq
    n_kv = tk_len // tk
    qb = min(MXU_N, 2 * tq)
    n_blk = 2 * tq // qb
    v_rows = HEAD_V + 16
    if q_off:
        assert nq == 1 and tk == tk_len and q_off % CHUNK == 0
    else:
        assert tq % tk == 0 and tq_len == tk_len and tq % CHUNK == 0
        assert nq == 1 or (tq // tk) % 2 == 0
    body = functools.partial(_attn_body, tq=tq, tk=tk, qb=qb, q_off=q_off,
                             lambda_init=lambda_init)
    qblk = pl.BlockSpec((tq, HEAD_V), lambda b, h, i: (b * nq + i, h))
    kvblk = pl.BlockSpec((1, tk_len, HEAD_V), lambda b, h, i: (b, 0, h))
    full = lambda a: pl.BlockSpec(a.shape, lambda b, h, i: (0,) * a.ndim)
    return pl.pallas_call(
        body,
        grid=(batch, heads, nq),
        in_specs=[full(lam4), full(subln), qblk, kvblk, kvblk, qblk],
        out_specs=qblk,
        out_shape=jax.ShapeDtypeStruct(q2.shape, BF16),
        scratch_shapes=[pltpu.VMEM((tk_len, HEAD_V), BF16),
                        pltpu.VMEM((n_kv, v_rows, tk), BF16),
                        pltpu.VMEM((n_blk, HEAD_V, qb), BF16),
                        pltpu.VMEM((n_blk, tk, qb), F32),
                        pltpu.VMEM((n_blk, tk, qb), F32),
                        pltpu.VMEM((n_blk, 1, qb), F32),
                        pltpu.VMEM((n_blk, 1, qb), F32),
                        pltpu.VMEM((n_blk, 1, qb), F32),
                        pltpu.VMEM((n_blk, v_rows, qb), F32)],
        compiler_params=pltpu.CompilerParams(
            dimension_semantics=("arbitrary", "arbitrary", "arbitrary"),
            vmem_limit_bytes=VMEM_LIMIT),
        name="diffattn",
    )(lam4, subln, q2, k3, v3, ga2)


def _rwkv_body(m_ref, gb_ref, s0_ref, wcomb_ref, w0_ref, a0_ref, kk_ref, ka_ref, rk_ref,
               lnw_ref, lnb_ref, yb_ref, sout_ref, s_scr, *, width):
    c = pl.program_id(1)
    L = RW_CHUNK
    gw = MXU_N
    hpg = gw // RW_HEAD
    n_grp = width // gw

    @pl.when(c == 0)
    def _():
        s_scr[...] = s0_ref[0]

    r = m_ref[:, 0:width].astype(F32)
    kb = m_ref[:, width:2 * width].astype(F32)
    vb = m_ref[:, 2 * width:3 * width].astype(F32)
    z = m_ref[:, 3 * width:3 * width + 2 * LORA].astype(F32)
    lane = lax.broadcasted_iota(jnp.int32, (L, 2 * LORA), 1)
    zt = jnp.where(lane < LORA, jnp.tanh(z), z).astype(BF16)
    lin = _dot(zt, wcomb_ref[...])
    w_log = -_softplus(-(w0_ref[...] + lin[:, :width])) - 0.5
    logw = -jnp.exp(w_log)
    alpha = _sigmoid(a0_ref[...] + lin[:, width:])

    gi = lax.broadcasted_iota(jnp.int32, (gw, gw), 0) // RW_HEAD
    gj = lax.broadcasted_iota(jnp.int32, (gw, gw), 1) // RW_HEAD
    same_head = gi == gj
    group_ones = same_head.astype(BF16)

    def head_sum(t):
        tb = t.astype(BF16)
        return jnp.concatenate([_dot(tb[:, o:o + gw], group_ones)
                                for o in range(0, width, gw)], axis=1)

    kk = kb * kk_ref[...]
    kk = kk / jnp.maximum(jnp.sqrt(head_sum(kk * kk)), 1e-12)
    kb = kb * (1.0 + (alpha - 1.0) * ka_ref[...])

    ti = lax.broadcasted_iota(jnp.int32, (L, L), 0)
    tj = lax.broadcasted_iota(jnp.int32, (L, L), 1)
    tri = (ti >= tj).astype(BF16)
    lw_hi = logw.astype(BF16)
    lw_lo = (logw - lw_hi.astype(F32)).astype(BF16)
    lp = _dot(tri, lw_hi) + _dot(tri, lw_lo)
    p_inv = jnp.exp(-lp)
    r_hat = r * jnp.exp(lp)
    a_hat = -kk * jnp.exp(lp - logw)
    b_til = kk * alpha * p_inv
    k_til = kb * p_inv
    p_last = jnp.exp(lp[L - 1:L, :])

    lane_head = lax.broadcasted_iota(jnp.int32, (L, gw), 1) // RW_HEAD

    def stack(t):
        tb = t.astype(BF16)
        zero = jnp.zeros_like(tb)
        return jnp.concatenate([jnp.where(lane_head == h, tb, zero) for h in range(hpg)], axis=0)

    row = lax.broadcasted_iota(jnp.int32, (L, gw), 0)
    col = lax.broadcasted_iota(jnp.int32, (L, gw), 1) % L
    strict = col < row
    incl = col <= row
    eye = (col == row).astype(F32)
    grp = [slice(g * gw, (g + 1) * gw) for g in range(n_grp)]

    ar = [jnp.concatenate([a_hat[:, s], r_hat[:, s]], axis=0).astype(BF16) for s in grp]
    bk = [jnp.concatenate([stack(b_til[:, s]), stack(k_til[:, s])], axis=0) for s in grp]
    gram = [_dot_nt(ar[g], bk[g]) for g in range(n_grp)]
    a_ab = [jnp.where(strict, gm[:L, :gw], 0.0) for gm in gram]
    a_ak = [jnp.where(strict, gm[:L, gw:], 0.0).astype(BF16) for gm in gram]
    a_r = [jnp.concatenate([jnp.where(incl, gm[L:, :gw], 0.0),
                            jnp.where(incl, gm[L:, gw:], 0.0)], axis=1).astype(BF16)
           for gm in gram]

    t_inv = [eye + a for a in a_ab]
    pw = [_dot(a.astype(BF16), stack(a)) for a in a_ab]
    for _ in range(int(math.log2(L)) - 2):
        res = [_dot(jnp.concatenate([pw[g], t_inv[g]], axis=0).astype(BF16), stack(pw[g]))
               for g in range(n_grp)]
        t_inv = [t_inv[g] + res[g][L:] for g in range(n_grp)]
        pw = [res[g][:L] for g in range(n_grp)]
    t_inv = [(t_inv[g] + _dot(t_inv[g].astype(BF16), stack(pw[g]))).astype(BF16)
             for g in range(n_grp)]

    s_old = [s_scr[g] for g in range(n_grp)]
    asr = [_dot_nt(ar[g], s_old[g].astype(BF16)) for g in range(n_grp)]
    v_s = [stack(vb[:, s]) for s in grp]
    x = [asr[g][:L] + _dot(a_ak[g], v_s[g]) for g in range(n_grp)]
    u = [_dot(t_inv[g], stack(x[g])) for g in range(n_grp)]
    ys = [asr[g][L:] + _dot(a_r[g], jnp.concatenate([stack(u[g]), v_s[g]], axis=0))
          for g in range(n_grp)]
    for g, s in enumerate(grp):
        upd = _dot_tn(jnp.concatenate([u[g], vb[:, s]], axis=0).astype(BF16),
                      jnp.concatenate([b_til[:, s], k_til[:, s]], axis=0).astype(BF16))
        s_scr[g] = (s_old[g] + jnp.where(same_head, upd, 0.0)) * p_last[:, s]
    yb = jnp.concatenate(ys, axis=1)

    inv_n = 1.0 / RW_HEAD
    mean = head_sum(yb) * inv_n
    d = yb - mean
    var = head_sum(d * d) * inv_n
    yn = d * lax.rsqrt(var + GN_EPS) * lnw_ref[...] + lnb_ref[...]
    bonus = head_sum(r * kb * rk_ref[...]) * vb
    yb_ref[...] = ((yn + bonus) * gb_ref[...].astype(F32)).astype(BF16)

    @pl.when(c == pl.num_programs(1) - 1)
    def _():
        sout_ref[0] = s_scr[...]


def _rwkv(m2, gb2, s0g, wcomb, w0, a0, k_k, k_a, r_k, ln_w, ln_b, *, batch, seq):
    rows, shift_w = m2.shape
    width = gb2.shape[1]
    nc = seq // RW_CHUNK
    n_grp = width // MXU_N
    body = functools.partial(_rwkv_body, width=width)
    row_blk = lambda w: pl.BlockSpec((RW_CHUNK, w), lambda b, c: (b * nc + c, 0))
    full = lambda a: pl.BlockSpec(a.shape, lambda b, c: (0,) * a.ndim)
    st = pl.BlockSpec((1, n_grp, MXU_N, MXU_N), lambda b, c: (b, 0, 0, 0))
    return pl.pallas_call(
        body,
        grid=(batch, nc),
        in_specs=[row_blk(shift_w), row_blk(width), st, full(wcomb), full(w0), full(a0),
                  full(k_k), full(k_a), full(r_k), full(ln_w), full(ln_b)],
        out_specs=[row_blk(width), st],
        out_shape=[jax.ShapeDtypeStruct((rows, width), BF16),
                   jax.ShapeDtypeStruct(s0g.shape, F32)],
        scratch_shapes=[pltpu.VMEM((n_grp, MXU_N, MXU_N), F32)],
        compiler_params=pltpu.CompilerParams(dimension_semantics=("arbitrary", "arbitrary"),
                                             vmem_limit_bytes=VMEM_LIMIT),
        name="rwkv7",
    )(m2, gb2, s0g, wcomb, w0, a0, k_k, k_a, r_k, ln_w, ln_b)


def _outproj_body(x_ref, ya_ref, yb_ref, wa_ref, wb_ref, npost_ref, y_ref):
    out = _dot(ya_ref[...], wa_ref[...]) + _dot(yb_ref[...], wb_ref[...])
    y = out * lax.rsqrt(jnp.mean(out * out, axis=-1, keepdims=True) + NORM_EPS)
    y_ref[...] = x_ref[...] + y * npost_ref[...]


def _outproj(x2, ya2, yb2, wa, wb, npost, *, tm):
    rows, d = x2.shape
    row_blk = lambda w: pl.BlockSpec((tm, w), lambda i: (i, 0))
    full = lambda a: pl.BlockSpec(a.shape, lambda i: (0,) * a.ndim)
    return pl.pallas_call(
        _outproj_body,
        grid=(rows // tm,),
        in_specs=[row_blk(d), row_blk(ya2.shape[1]), row_blk(yb2.shape[1]),
                  full(wa), full(wb), full(npost)],
        out_specs=row_blk(d),
        out_shape=jax.ShapeDtypeStruct((rows, d), F32),
        compiler_params=pltpu.CompilerParams(dimension_semantics=("arbitrary",),
                                             vmem_limit_bytes=VMEM_LIMIT),
        name="outproj",
    )(x2, ya2, yb2, wa, wb, npost)


def _rope_tables(pos):
    half = ROT_DIM // 2
    inv = jnp.power(jnp.float32(ROPE_THETA), -jnp.arange(half, dtype=F32) * (2.0 / ROT_DIM))
    ang = pos.astype(F32)[:, None] * inv[None, :]
    cos, sin = jnp.cos(ang), jnp.sin(ang)
    n = pos.shape[0]
    pad = HEAD_QK - ROT_DIM
    zeros, ones = jnp.zeros((n, half), F32), jnp.ones((n, pad), F32)
    zpad = jnp.zeros((n, pad), F32)
    c = jnp.concatenate([cos, cos, ones], axis=1)
    s_prev = jnp.concatenate([zeros, sin, zpad], axis=1)
    s_next = jnp.concatenate([-sin, zeros, zpad], axis=1)
    rep = LANES // HEAD_QK
    return tuple(jnp.tile(t, (1, rep)) for t in (c, s_prev, s_next))


def _group_states(s):
    b, h, n, _ = s.shape
    hpg = MXU_N // n
    s = s.reshape(b, h // hpg, hpg, n, n)
    out = jnp.einsum("bghij,hk->bghikj", s, jnp.eye(hpg, dtype=s.dtype))
    return out.reshape(b, h // hpg, MXU_N, MXU_N)


def _ungroup_states(sg):
    n = RW_HEAD
    hpg = MXU_N // n
    b, g = sg.shape[:2]
    s = jnp.einsum("bghikj,hk->bghij", sg.reshape(b, g, hpg, n, hpg, n), jnp.eye(hpg, dtype=sg.dtype))
    return s.reshape(b, g * hpg, n, n)


def _stream(x, pos, past_k, past_v, wkv0, shift0, p, lambda_init):
    batch, seq, d = x.shape
    width = p["w_out_a"].shape[0]
    heads = width // HEAD_V
    rows = batch * seq
    x2 = x.reshape(rows, d)
    tm = min(256, seq)
    cos, s1, s2 = _rope_tables(pos)
    q2, k2, v2, ga2, m2, gb2, shift_out = _inproj(
        x2, shift0, p["norm_pre"], p["mu_shift"], cos, s1, s2, p["w_in"],
        batch=batch, seq=seq, tm=tm)

    k3 = k2.reshape(batch, seq, width)
    v3 = v2.reshape(batch, seq, width)
    if past_k is None:
        k_all, v_all, q_off = k3, v3, 0
        tq, tk = min(512, seq), min(256, seq)
    else:
        past = past_k.shape[1]
        total = past + seq
        padded = -(-total // LANES) * LANES
        zpad = jnp.zeros((batch, padded - total, width), F32)
        k_all = jnp.concatenate([past_k.reshape(batch, past, width), k3, zpad], axis=1)
        v_all = jnp.concatenate([past_v.reshape(batch, past, width), v3, zpad], axis=1)
        q_off, tq, tk = past, seq, padded
    ya2 = _attention(p["lam4"], p["subln"], q2, k_all, v_all, ga2, batch=batch, tq_len=seq,
                     tq=tq, tk=tk, q_off=q_off, lambda_init=lambda_init)

    yb2, s_groups = _rwkv(m2, gb2, _group_states(wkv0), p["w_comb"], p["w0"], p["a0"], p["k_k"],
                          p["k_a"], p["r_k"], p["ln_x_w"], p["ln_x_b"], batch=batch, seq=seq)

    y2 = _outproj(x2, ya2, yb2, p["w_out_a"], p["w_out_b"], p["norm_post"], tm=tm)
    return (y2.reshape(batch, seq, d),
            k3.reshape(batch, seq, heads, 2, HEAD_QK),
            v3.reshape(batch, seq, heads, HEAD_V),
            _ungroup_states(s_groups),
            shift_out)


def kernel(x_prompt, x_sample, cache_k, cache_v, state_wkv, state_shift, norm_pre, w_in, lam_q1, lam_k1, lam_q2, lam_k2, subln, mu_shift, w0, w_up, a0, a_up, k_k, k_a, r_k, ln_x_w, ln_x_b, w_out, norm_post):
    depth = w_in.shape[0]
    assert depth == 1, "single-layer problem"
    l = 0
    lambda_init = 0.8 - 0.6 * math.exp(-0.3 * l)
    width = w_up.shape[2]
    row = lambda a: a[l].reshape(1, -1).astype(F32)
    zeros = jnp.zeros((LORA, width), F32)
    p = {
        "norm_pre": row(norm_pre), "norm_post": row(norm_post), "subln": row(subln),
        "mu_shift": row(mu_shift), "w0": row(w0), "a0": row(a0), "k_k": row(k_k), "k_a": row(k_a),
        "r_k": row(r_k), "ln_x_w": row(ln_x_w), "ln_x_b": row(ln_x_b),
        "w_in": w_in[l].astype(BF16),
        "w_out_a": w_out[l, :width].astype(BF16), "w_out_b": w_out[l, width:].astype(BF16),
        "w_comb": jnp.concatenate([jnp.concatenate([w_up[l], zeros], axis=1),
                                   jnp.concatenate([zeros, a_up[l]], axis=1)], axis=0).astype(BF16),
        "lam4": jnp.stack([lam_q1[l], lam_k1[l], lam_q2[l], lam_k2[l]]).astype(F32),
    }
    b_p, t_p, _ = x_prompt.shape
    b_s, t_s, _ = x_sample.shape
    past_len = cache_k.shape[2]
    shift_w = state_shift.shape[-1]
    n_heads_b = state_wkv.shape[2]
    pos_p = jnp.arange(t_p, dtype=jnp.int32)
    pos_s = past_len + jnp.arange(t_s, dtype=jnp.int32)

    yp, kp, vp, wp, sp = _stream(
        x_prompt, pos_p, None, None,
        jnp.zeros((b_p, n_heads_b, RW_HEAD, RW_HEAD), F32), jnp.zeros((b_p, 1, shift_w), F32),
        p, lambda_init)
    ys, ks, vs, ws, ss = _stream(
        x_sample, pos_s, cache_k[l], cache_v[l], state_wkv[l].astype(F32), state_shift[l],
        p, lambda_init)
    return (yp, ys, kp[None], vp[None], wp[None], sp[None],
            ks[None], vs[None], ws[None].astype(state_wkv.dtype), ss[None])
```

```python
import functools
import math

import jax
import jax.numpy as jnp
from jax import lax
from jax.experimental import pallas as pl
from jax.experimental.pallas import tpu as pltpu

F32 = jnp.float32
BF16 = jnp.bfloat16

CHUNK = 64
HEAD_V = 128
HEAD_QK = 64
ROT_DIM = 16
ROPE_THETA = 500000.0
RW_HEAD = 64
RW_CHUNK = 64
LORA = 64
NORM_EPS = 1e-6
SUBLN_EPS = 1e-5
GN_EPS = 64e-5
LANES = 128
MXU_N = 256
VMEM_LIMIT = 56 * 1024 * 1024
Q_SCALE = HEAD_QK ** -0.5 * math.log2(math.e)


def _sigmoid(x):
    return 1.0 / (1.0 + jnp.exp(-x))


def _softplus(x):
    return jnp.maximum(x, 0.0) + jnp.log1p(jnp.exp(-jnp.abs(x)))


def _dot(a, b):
    return jnp.dot(a, b, preferred_element_type=F32)


def _dot_nt(a, b):
    return lax.dot_general(a, b, (((1,), (1,)), ((), ())), preferred_element_type=F32)


def _dot_tn(a, b):
    return lax.dot_general(a, b, (((0,), (0,)), ((), ())), preferred_element_type=F32)


def _inproj_body(x_ref, shift0_ref, npre_ref, mu_ref, cos_ref, s1_ref, s2_ref, w_ref,
                 q_ref, k_ref, kb_ref, v_ref, ga_ref, m_ref, gb_ref, shift_ref, h_scr, carry_scr,
                 *, tiles_per_batch, tm, width, shift_w, col_chunk, k_transposed):
    i = pl.program_id(0)

    @pl.when(i == 0)
    def _():
        carry_scr[...] = jnp.zeros_like(carry_scr)

    x = x_ref[...]
    y = x * lax.rsqrt(jnp.mean(x * x, axis=-1, keepdims=True) + NORM_EPS)
    h_scr[...] = (y * npre_ref[...]).astype(BF16)

    def proj(c0, cw):
        return _dot(h_scr[...], w_ref[:, c0:c0 + cw])

    cos = cos_ref[...]
    s1 = s1_ref[...]
    s2 = s2_ref[...]

    def rope(p):
        outs = []
        for g in range(p.shape[1] // LANES):
            xg = p[:, g * LANES:(g + 1) * LANES]
            outs.append(xg * cos + pltpu.roll(xg, ROT_DIM // 2, axis=1) * s1
                        + pltpu.roll(xg, LANES - ROT_DIM // 2, axis=1) * s2)
        return jnp.concatenate(outs, axis=1)

    n_sub = width // col_chunk
    a_q, a_k, a_v, a_g = 0, width, 2 * width, 3 * width
    b_s = 4 * width
    b_g = b_s + shift_w
    for c in range(n_sub):
        o = c * col_chunk
        sl = slice(o, o + col_chunk)
        q_ref[:, sl] = (rope(proj(a_q + o, col_chunk)) * Q_SCALE).astype(BF16)
        k = rope(proj(a_k + o, col_chunk))
        kb_ref[:, sl] = k.astype(BF16)
        if k_transposed:
            k_ref[0, sl, :] = k.T
        else:
            k_ref[:, sl] = k
        v_ref[:, sl] = proj(a_v + o, col_chunk)
        g = proj(a_g + o, col_chunk)
        ga_ref[:, sl] = (g * _sigmoid(g)).astype(BF16)
        g = proj(b_g + o, col_chunk)
        gb_ref[:, sl] = (g * _sigmoid(g)).astype(BF16)

    is_first = (i % tiles_per_batch) == 0
    o = 0
    while o < shift_w:
        cw = min(col_chunk, shift_w - o)
        sl = slice(o, o + cw)
        ps = proj(b_s + o, cw)
        prev_row = jnp.where(is_first, shift0_ref[0, :, sl], carry_scr[:, sl])
        rows = lax.broadcasted_iota(jnp.int32, ps.shape, 0)
        prev = jnp.where(rows == 0, prev_row, pltpu.roll(ps, 1, axis=0))
        last = ps[tm - 1:tm, :]
        carry_scr[:, sl] = last
        shift_ref[0, :, sl] = last
        m_ref[:, sl] = (ps + (prev - ps) * mu_ref[:, sl]).astype(BF16)
        o += cw


def _inproj(x2, shift0, npre, mu, cos, s1, s2, w_bf, *, batch, seq, tm):
    rows, d = x2.shape
    width = (w_bf.shape[1] - 2 * LORA) // 8
    shift_w = 3 * width + 2 * LORA
    tpb = seq // tm
    k_transposed = tm % LANES == 0
    body = functools.partial(_inproj_body, tiles_per_batch=tpb, tm=tm, width=width,
                             shift_w=shift_w, col_chunk=512, k_transposed=k_transposed)
    if k_transposed:
        k_spec = pl.BlockSpec((1, width, tm), lambda i: (i // tpb, 0, i % tpb))
        k_shape = jax.ShapeDtypeStruct((batch, width, seq), F32)
    else:
        k_spec = pl.BlockSpec((tm, width), lambda i: (i, 0))
        k_shape = jax.ShapeDtypeStruct((rows, width), F32)
    row_blk = lambda w: pl.BlockSpec((tm, w), lambda i: (i, 0))
    full = lambda a: pl.BlockSpec(a.shape, lambda i: (0,) * a.ndim)
    tab = pl.BlockSpec((tm, LANES), lambda i: (i % tpb, 0))
    per_b = pl.BlockSpec((1, 1, shift_w), lambda i: (i // tpb, 0, 0))
    return pl.pallas_call(
        body,
        grid=(rows // tm,),
        in_specs=[row_blk(d), per_b, full(npre), full(mu), tab, tab, tab,
                  pl.BlockSpec(memory_space=pltpu.VMEM)],
        out_specs=[row_blk(width), k_spec, row_blk(width), row_blk(width), row_blk(width),
                   row_blk(shift_w), row_blk(width), per_b],
        out_shape=[jax.ShapeDtypeStruct((rows, width), BF16),
                   k_shape,
                   jax.ShapeDtypeStruct((rows, width), BF16),
                   jax.ShapeDtypeStruct((rows, width), F32),
                   jax.ShapeDtypeStruct((rows, width), BF16),
                   jax.ShapeDtypeStruct((rows, shift_w), BF16),
                   jax.ShapeDtypeStruct((rows, width), BF16),
                   jax.ShapeDtypeStruct((batch, 1, shift_w), F32)],
        scratch_shapes=[pltpu.VMEM((tm, d), BF16), pltpu.VMEM((1, shift_w), F32)],
        compiler_params=pltpu.CompilerParams(dimension_semantics=("arbitrary",),
                                             vmem_limit_bytes=VMEM_LIMIT),
        name="inproj",
    )(x2, shift0, npre, mu, cos, s1, s2, w_bf)


def _attn_body(lam_ref, subln_ref, q_ref, k_ref, v_ref, ga_ref, ya_ref,
               vt_scr, qt_scr, sa_scr, sb_scr, cma_scr, cmb_scr, m_scr, acc_scr,
               *, tq, tk, qb, q_off, lambda_init):
    qi = pl.program_id(2)
    n_kv = vt_scr.shape[0]
    n_blk = 2 * tq // qb
    n_diag = max(tq // tk, 1)
    v_rows = vt_scr.shape[1]

    @pl.when(qi == 0)
    def _():
        def cvt(c, carry):
            sl = pl.ds(pl.multiple_of(c * tk, tk), tk)
            vt_scr[c, 0:HEAD_V, :] = v_ref[0, sl, :].T.astype(BF16)
            vt_scr[c, HEAD_V:, :] = jnp.ones((v_rows - HEAD_V, tk), BF16)
            return carry
        lax.fori_loop(0, n_kv, cvt, 0)

    q = q_ref[...].astype(F32)
    lane = lax.broadcasted_iota(jnp.int32, q.shape, 1)
    qs = jnp.concatenate([jnp.where(lane < HEAD_QK, q, 0.0),
                          jnp.where(lane >= HEAD_QK, q, 0.0)], axis=0)
    for c in range(n_blk):
        qt_scr[c] = qs[c * qb:(c + 1) * qb].T.astype(BF16)

    m_scr[...] = jnp.full_like(m_scr, -jnp.inf)
    acc_scr[...] = jnp.zeros_like(acc_scr)

    def diag_kind(t, c):
        q_lo = (c * qb) % tq
        if qb <= tq and q_lo + qb <= t * tk:
            return "skip"
        if qb <= tq and q_off == 0 and q_lo >= (t + 1) * tk:
            return None
        qrel = (c * qb + lax.broadcasted_iota(jnp.int32, (tk, qb), 1)) % tq
        krel = t * tk + lax.broadcasted_iota(jnp.int32, (tk, qb), 0)
        if q_off:
            return krel // CHUNK <= (q_off + qrel) // CHUNK
        return krel // CHUNK <= qrel // CHUNK

    bufs = ((sa_scr, cma_scr), (sb_scr, cmb_scr))

    def scores(j, buf, kinds):
        s_ref, cm_ref = bufs[buf]
        k_j = k_ref[0, pl.ds(pl.multiple_of(j * tk, tk), tk), :]
        for c in range(n_blk):
            if isinstance(kinds[c], str):
                continue
            s = _dot(k_j, qt_scr[c])
            if kinds[c] is not None:
                s = jnp.where(kinds[c], s, -jnp.inf)
            s_ref[c] = s
            cm_ref[c] = jnp.max(s, axis=0, keepdims=True)

    def consume(j, buf, kinds):
        s_ref, cm_ref = bufs[buf]
        vt_j = vt_scr[j]
        for c in range(n_blk):
            if isinstance(kinds[c], str):
                continue
            m_prev = m_scr[c]
            m_new = jnp.maximum(m_prev, cm_ref[c])
            alpha = jnp.exp2(m_prev - m_new)
            p = jnp.exp2(s_ref[c] - m_new).astype(BF16)
            acc_scr[c] = alpha * acc_scr[c] + _dot(vt_j, p)
            m_scr[c] = m_new

    all_visible = [None] * n_blk
    diag = [[diag_kind(t, c) for c in range(n_blk)] for t in range(n_diag)]
    n_full = (qi * tq) // tk

    @pl.when(n_full > 0)
    def _():
        scores(0, 0, all_visible)

    @pl.when(n_full == 0)
    def _():
        scores(0, 0, diag[0])

    def tile_pair(j, next_kinds):
        scores(j + 1, 1, all_visible)
        consume(j, 0, all_visible)
        scores(j + 2, 0, next_kinds)
        consume(j + 1, 1, all_visible)

    def quad_step(i, carry):
        tile_pair(4 * i, all_visible)
        tile_pair(4 * i + 2, all_visible)
        return carry

    n_pairs = n_full // 2
    lax.fori_loop(0, (n_pairs - 1) // 2, quad_step, 0)

    @pl.when(jnp.logical_and(n_pairs > 0, n_pairs % 2 == 0))
    def _():
        tile_pair(n_full - 4, all_visible)

    @pl.when(n_full > 0)
    def _():
        tile_pair(n_full - 2, diag[0])

    for t in range(n_diag):
        if t + 1 < n_diag:
            scores(n_full + t + 1, (t + 1) % 2, diag[t + 1])
        consume(n_full + t, t % 2, diag[t])

    lam4 = lam_ref[...]
    e1 = jnp.exp(jnp.sum(lam4[0:1] * lam4[1:2], axis=1, keepdims=True))
    e2 = jnp.exp(jnp.sum(lam4[2:3] * lam4[3:4], axis=1, keepdims=True))
    lam = e1 - e2 + lambda_init
    acc = jnp.concatenate([acc_scr[c] for c in range(n_blk)], axis=1)
    o_all = acc[0:HEAD_V] / acc[HEAD_V:HEAD_V + 1]
    o = (o_all[:, :tq] - lam * o_all[:, tq:]).T
    y = o * lax.rsqrt(jnp.mean(o * o, axis=-1, keepdims=True) + SUBLN_EPS)
    y = (y * subln_ref[...]) * (1.0 - lambda_init)
    ya_ref[...] = (y * ga_ref[...].astype(F32)).astype(BF16)


def _attention(lam4, subln, q2, k3, v3, ga2, *, batch, tq_len, tq, tk, q_off, lambda_init):
    tk_len = k3.shape[1]
    heads = k3.shape[2] // HEAD_V
    nq = tq_len // tq
    n_kv = tk_len // tk
    qb = min(MXU_N, 2 * tq)
    n_blk = 2 * tq // qb
    v_rows = HEAD_V + 16
    if q_off:
        assert nq == 1 and tk == tk_len and q_off % CHUNK == 0
    else:
        assert tq % tk == 0 and tq_len == tk_len and tq % CHUNK == 0
        assert nq == 1 or (tq // tk) % 2 == 0
    body = functools.partial(_attn_body, tq=tq, tk=tk, qb=qb, q_off=q_off,
                             lambda_init=lambda_init)
    qblk = pl.BlockSpec((tq, HEAD_V), lambda b, h, i: (b * nq + i, h))
    kvblk = pl.BlockSpec((1, tk_len, HEAD_V), lambda b, h, i: (b, 0, h))
    full = lambda a: pl.BlockSpec(a.shape, lambda b, h, i: (0,) * a.ndim)
    return pl.pallas_call(
        body,
        grid=(batch, heads, nq),
        in_specs=[full(lam4), full(subln), qblk, kvblk, kvblk, qblk],
        out_specs=qblk,
        out_shape=jax.ShapeDtypeStruct(q2.shape, BF16),
        scratch_shapes=[pltpu.VMEM((n_kv, v_rows, tk), BF16),
                        pltpu.VMEM((n_blk, HEAD_V, qb), BF16),
                        pltpu.VMEM((n_blk, tk, qb), F32),
                        pltpu.VMEM((n_blk, tk, qb), F32),
                        pltpu.VMEM((n_blk, 1, qb), F32),
                        pltpu.VMEM((n_blk, 1, qb), F32),
                        pltpu.VMEM((n_blk, 1, qb), F32),
                        pltpu.VMEM((n_blk, v_rows, qb), F32)],
        compiler_params=pltpu.CompilerParams(
            dimension_semantics=("arbitrary", "arbitrary", "arbitrary"),
            vmem_limit_bytes=VMEM_LIMIT),
        name="diffattn",
    )(lam4, subln, q2, k3, v3, ga2)


def _rwkv_body(m_ref, gb_ref, s0_ref, wcomb_ref, w0_ref, a0_ref, kk_ref, ka_ref, rk_ref,
               lnw_ref, lnb_ref, yb_ref, sout_ref, s_scr, *, width):
    c = pl.program_id(1)
    L = RW_CHUNK
    gw = MXU_N
    hpg = gw // RW_HEAD
    n_grp = width // gw

    @pl.when(c == 0)
    def _():
        s_scr[...] = s0_ref[0]

    r = m_ref[:, 0:width].astype(F32)
    kb = m_ref[:, width:2 * width].astype(F32)
    vb = m_ref[:, 2 * width:3 * width].astype(F32)
    z = m_ref[:, 3 * width:3 * width + 2 * LORA].astype(F32)
    lane = lax.broadcasted_iota(jnp.int32, (L, 2 * LORA), 1)
    zt = jnp.where(lane < LORA, jnp.tanh(z), z).astype(BF16)
    lin = _dot(zt, wcomb_ref[...])
    w_log = -_softplus(-(w0_ref[...] + lin[:, :width])) - 0.5
    logw = -jnp.exp(w_log)
    alpha = _sigmoid(a0_ref[...] + lin[:, width:])

    gi = lax.broadcasted_iota(jnp.int32, (gw, gw), 0) // RW_HEAD
    gj = lax.broadcasted_iota(jnp.int32, (gw, gw), 1) // RW_HEAD
    same_head = gi == gj
    group_ones = same_head.astype(BF16)

    def head_sums(*ts):
        rows = jnp.concatenate([t[:, o:o + gw].astype(BF16) for t in ts
                                for o in range(0, width, gw)], axis=0)
        out = _dot(rows, group_ones)
        return [jnp.concatenate([out[(i * n_grp + g) * L:(i * n_grp + g + 1) * L]
                                 for g in range(n_grp)], axis=1) for i in range(len(ts))]

    kk = kb * kk_ref[...]
    kb = kb * (1.0 + (alpha - 1.0) * ka_ref[...])
    kk_sq, bonus_dot = head_sums(kk * kk, r * kb * rk_ref[...])
    kk = kk / jnp.maximum(jnp.sqrt(kk_sq), 1e-12)

    ti = lax.broadcasted_iota(jnp.int32, (L, L), 0)
    tj = lax.broadcasted_iota(jnp.int32, (L, L), 1)
    tri = (ti >= tj).astype(BF16)
    lw_hi = logw.astype(BF16)
    lw_lo = (logw - lw_hi.astype(F32)).astype(BF16)
    lp = _dot(tri, lw_hi) + _dot(tri, lw_lo)
    p_inv = jnp.exp(-lp)
    r_hat = r * jnp.exp(lp)
    a_hat = -kk * jnp.exp(lp - logw)
    b_til = kk * alpha * p_inv
    k_til = kb * p_inv
    p_last = jnp.exp(lp[L - 1:L, :])

    lane_head = lax.broadcasted_iota(jnp.int32, (L, gw), 1) // RW_HEAD

    def stack(t):
        tb = t.astype(BF16)
        zero = jnp.zeros_like(tb)
        return jnp.concatenate([jnp.where(lane_head == h, tb, zero) for h in range(hpg)], axis=0)

    row = lax.broadcasted_iota(jnp.int32, (L, gw), 0)
    col = lax.broadcasted_iota(jnp.int32, (L, gw), 1) % L
    strict = col < row
    incl = col <= row
    eye = (col == row).astype(F32)
    grp = [slice(g * gw, (g + 1) * gw) for g in range(n_grp)]

    ar = [jnp.concatenate([a_hat[:, s], r_hat[:, s]], axis=0).astype(BF16) for s in grp]
    bk = [jnp.concatenate([stack(b_til[:, s]), stack(k_til[:, s])], axis=0) for s in grp]
    gram = [_dot_nt(ar[g], bk[g]) for g in range(n_grp)]
    a_ab = [jnp.where(strict, gm[:L, :gw], 0.0) for gm in gram]
    a_ak = [jnp.where(strict, gm[:L, gw:], 0.0).astype(BF16) for gm in gram]
    a_r = [jnp.concatenate([jnp.where(incl, gm[L:, :gw], 0.0),
                            jnp.where(incl, gm[L:, gw:], 0.0)], axis=1).astype(BF16)
           for gm in gram]

    t_inv = [eye + a for a in a_ab]
    pw = [_dot(a.astype(BF16), stack(a)) for a in a_ab]
    for _ in range(int(math.log2(L)) - 2):
        res = [_dot(jnp.concatenate([pw[g], t_inv[g]], axis=0).astype(BF16), stack(pw[g]))
               for g in range(n_grp)]
        t_inv = [t_inv[g] + res[g][L:] for g in range(n_grp)]
        pw = [res[g][:L] for g in range(n_grp)]
    t_inv = [(t_inv[g] + _dot(t_inv[g].astype(BF16), stack(pw[g]))).astype(BF16)
             for g in range(n_grp)]

    s_old = [s_scr[g] for g in range(n_grp)]
    asr = [_dot_nt(ar[g], s_old[g].astype(BF16)) for g in range(n_grp)]
    v_s = [stack(vb[:, s]) for s in grp]
    x = [asr[g][:L] + _dot(a_ak[g], v_s[g]) for g in range(n_grp)]
    u = [_dot(t_inv[g], stack(x[g])) for g in range(n_grp)]
    ys = [asr[g][L:] + _dot(a_r[g], jnp.concatenate([stack(u[g]), v_s[g]], axis=0))
          for g in range(n_grp)]
    for g, s in enumerate(grp):
        upd = _dot_tn(jnp.concatenate([u[g], vb[:, s]], axis=0).astype(BF16),
                      jnp.concatenate([b_til[:, s], k_til[:, s]], axis=0).astype(BF16))
        s_scr[g] = (s_old[g] + jnp.where(same_head, upd, 0.0)) * p_last[:, s]
    yb = jnp.concatenate(ys, axis=1)

    inv_n = 1.0 / RW_HEAD
    mean = head_sums(yb)[0] * inv_n
    d = yb - mean
    var = head_sums(d * d)[0] * inv_n
    yn = d * lax.rsqrt(var + GN_EPS) * lnw_ref[...] + lnb_ref[...]
    yb_ref[...] = ((yn + bonus_dot * vb) * gb_ref[...].astype(F32)).astype(BF16)

    @pl.when(c == pl.num_programs(1) - 1)
    def _():
        sout_ref[0] = s_scr[...]


def _rwkv(m2, gb2, s0g, wcomb, w0, a0, k_k, k_a, r_k, ln_w, ln_b, *, batch, seq):
    rows, shift_w = m2.shape
    width = gb2.shape[1]
    nc = seq // RW_CHUNK
    n_grp = width // MXU_N
    body = functools.partial(_rwkv_body, width=width)
    row_blk = lambda w: pl.BlockSpec((RW_CHUNK, w), lambda b, c: (b * nc + c, 0))
    full = lambda a: pl.BlockSpec(a.shape, lambda b, c: (0,) * a.ndim)
    st = pl.BlockSpec((1, n_grp, MXU_N, MXU_N), lambda b, c: (b, 0, 0, 0))
    return pl.pallas_call(
        body,
        grid=(batch, nc),
        in_specs=[row_blk(shift_w), row_blk(width), st, full(wcomb), full(w0), full(a0),
                  full(k_k), full(k_a), full(r_k), full(ln_w), full(ln_b)],
        out_specs=[row_blk(width), st],
        out_shape=[jax.ShapeDtypeStruct((rows, width), BF16),
                   jax.ShapeDtypeStruct(s0g.shape, F32)],
        scratch_shapes=[pltpu.VMEM((n_grp, MXU_N, MXU_N), F32)],
        compiler_params=pltpu.CompilerParams(dimension_semantics=("arbitrary", "arbitrary"),
                                             vmem_limit_bytes=VMEM_LIMIT),
        name="rwkv7",
    )(m2, gb2, s0g, wcomb, w0, a0, k_k, k_a, r_k, ln_w, ln_b)


def _outproj_body(x_ref, ya_ref, yb_ref, wa_ref, wb_ref, npost_ref, y_ref):
    out = _dot(ya_ref[...], wa_ref[...]) + _dot(yb_ref[...], wb_ref[...])
    y = out * lax.rsqrt(jnp.mean(out * out, axis=-1, keepdims=True) + NORM_EPS)
    y_ref[...] = x_ref[...] + y * npost_ref[...]


def _outproj(x2, ya2, yb2, wa, wb, npost, *, tm):
    rows, d = x2.shape
    row_blk = lambda w: pl.BlockSpec((tm, w), lambda i: (i, 0))
    full = lambda a: pl.BlockSpec(a.shape, lambda i: (0,) * a.ndim)
    return pl.pallas_call(
        _outproj_body,
        grid=(rows // tm,),
        in_specs=[row_blk(d), row_blk(ya2.shape[1]), row_blk(yb2.shape[1]),
                  full(wa), full(wb), full(npost)],
        out_specs=row_blk(d),
        out_shape=jax.ShapeDtypeStruct((rows, d), F32),
        compiler_params=pltpu.CompilerParams(dimension_semantics=("arbitrary",),
                                             vmem_limit_bytes=VMEM_LIMIT),
        name="outproj",
    )(x2, ya2, yb2, wa, wb, npost)


def _rope_tables(pos):
    half = ROT_DIM // 2
    inv = jnp.power(jnp.float32(ROPE_THETA), -jnp.arange(half, dtype=F32) * (2.0 / ROT_DIM))
    ang = pos.astype(F32)[:, None] * inv[None, :]
    cos, sin = jnp.cos(ang), jnp.sin(ang)
    n = pos.shape[0]
    pad = HEAD_QK - ROT_DIM
    zeros, ones = jnp.zeros((n, half), F32), jnp.ones((n, pad), F32)
    zpad = jnp.zeros((n, pad), F32)
    c = jnp.concatenate([cos, cos, ones], axis=1)
    s_prev = jnp.concatenate([zeros, sin, zpad], axis=1)
    s_next = jnp.concatenate([-sin, zeros, zpad], axis=1)
    rep = LANES // HEAD_QK
    return tuple(jnp.tile(t, (1, rep)) for t in (c, s_prev, s_next))


def _group_states(s):
    b, h, n, _ = s.shape
    hpg = MXU_N // n
    s = s.reshape(b, h // hpg, hpg, n, n)
    out = jnp.einsum("bghij,hk->bghikj", s, jnp.eye(hpg, dtype=s.dtype))
    return out.reshape(b, h // hpg, MXU_N, MXU_N)


def _ungroup_states(sg):
    n = RW_HEAD
    hpg = MXU_N // n
    b, g = sg.shape[:2]
    s = jnp.einsum("bghikj,hk->bghij", sg.reshape(b, g, hpg, n, hpg, n), jnp.eye(hpg, dtype=sg.dtype))
    return s.reshape(b, g * hpg, n, n)


def _stream(x, pos, past_k, past_v, wkv0, shift0, p, lambda_init):
    batch, seq, d = x.shape
    width = p["w_out_a"].shape[0]
    heads = width // HEAD_V
    rows = batch * seq
    x2 = x.reshape(rows, d)
    tm = min(256, seq)
    cos, s1, s2 = _rope_tables(pos)
    q2, k_out, kb2, v2, ga2, m2, gb2, shift_out = _inproj(
        x2, shift0, p["norm_pre"], p["mu_shift"], cos, s1, s2, p["w_in"],
        batch=batch, seq=seq, tm=tm)

    if k_out.shape[0] == batch:
        k_out = k_out.reshape(batch, heads, 2, HEAD_QK, seq).transpose(0, 4, 1, 2, 3)
    else:
        k_out = k_out.reshape(batch, seq, heads, 2, HEAD_QK)
    k3 = kb2.reshape(batch, seq, width)
    v3 = v2.reshape(batch, seq, width)
    if past_k is None:
        k_all, v_all, q_off = k3, v3, 0
        tq, tk = min(512, seq), min(256, seq)
    else:
        past = past_k.shape[1]
        total = past + seq
        padded = -(-total // LANES) * LANES
        zpad = jnp.zeros((batch, padded - total, width), F32)
        k_all = jnp.concatenate([past_k.reshape(batch, past, width).astype(BF16), k3,
                                 zpad.astype(BF16)], axis=1)
        v_all = jnp.concatenate([past_v.reshape(batch, past, width), v3, zpad], axis=1)
        q_off, tq, tk = past, seq, padded
    ya2 = _attention(p["lam4"], p["subln"], q2, k_all, v_all, ga2, batch=batch, tq_len=seq,
                     tq=tq, tk=tk, q_off=q_off, lambda_init=lambda_init)

    yb2, s_groups = _rwkv(m2, gb2, _group_states(wkv0), p["w_comb"], p["w0"], p["a0"], p["k_k"],
                          p["k_a"], p["r_k"], p["ln_x_w"], p["ln_x_b"], batch=batch, seq=seq)

    y2 = _outproj(x2, ya2, yb2, p["w_out_a"], p["w_out_b"], p["norm_post"], tm=tm)
    return (y2.reshape(batch, seq, d),
            k_out,
            v3.reshape(batch, seq, heads, HEAD_V),
            _ungroup_states(s_groups),
            shift_out)


def kernel(x_prompt, x_sample, cache_k, cache_v, state_wkv, state_shift, norm_pre, w_in, lam_q1, lam_k1, lam_q2, lam_k2, subln, mu_shift, w0, w_up, a0, a_up, k_k, k_a, r_k, ln_x_w, ln_x_b, w_out, norm_post):
    depth = w_in.shape[0]
    assert depth == 1, "single-layer problem"
    l = 0
    lambda_init = 0.8 - 0.6 * math.exp(-0.3 * l)
    width = w_up.shape[2]
    row = lambda a: a[l].reshape(1, -1).astype(F32)
    zeros = jnp.zeros((LORA, width), F32)
    p = {
        "norm_pre": row(norm_pre), "norm_post": row(norm_post), "subln": row(subln),
        "mu_shift": row(mu_shift), "w0": row(w0), "a0": row(a0), "k_k": row(k_k), "k_a": row(k_a),
        "r_k": row(r_k), "ln_x_w": row(ln_x_w), "ln_x_b": row(ln_x_b),
        "w_in": w_in[l].astype(BF16),
        "w_out_a": w_out[l, :width].astype(BF16), "w_out_b": w_out[l, width:].astype(BF16),
        "w_comb": jnp.concatenate([jnp.concatenate([w_up[l], zeros], axis=1),
                                   jnp.concatenate([zeros, a_up[l]], axis=1)], axis=0).astype(BF16),
        "lam4": jnp.stack([lam_q1[l], lam_k1[l], lam_q2[l], lam_k2[l]]).astype(F32),
    }
    b_p, t_p, _ = x_prompt.shape
    b_s, t_s, _ = x_sample.shape
    past_len = cache_k.shape[2]
    shift_w = state_shift.shape[-1]
    n_heads_b = state_wkv.shape[2]
    pos_p = jnp.arange(t_p, dtype=jnp.int32)
    pos_s = past_len + jnp.arange(t_s, dtype=jnp.int32)

    yp, kp, vp, wp, sp = _stream(
        x_prompt, pos_p, None, None,
        jnp.zeros((b_p, n_heads_b, RW_HEAD, RW_HEAD), F32), jnp.zeros((b_p, 1, shift_w), F32),
        p, lambda_init)
    ys, ks, vs, ws, ss = _stream(
        x_sample, pos_s, cache_k[l], cache_v[l], state_wkv[l].astype(F32), state_shift[l],
        p, lambda_init)
    return (yp, ys, kp[None], vp[None], wp[None], sp[None],
            ks[None], vs[None], ws[None].astype(state_wkv.dtype), ss[None])
```

```python
import functools
import itertools
import math

import jax
import jax.numpy as jnp
from jax import lax
from jax.experimental import pallas as pl
from jax.experimental.pallas import tpu as pltpu

F32 = jnp.float32
BF16 = jnp.bfloat16

CHUNK = 64
HEAD_V = 128
HEAD_QK = 64
ROT_DIM = 16
ROPE_THETA = 500000.0
RW_HEAD = 64
RW_CHUNK = 64
LORA = 64
NORM_EPS = 1e-6
SUBLN_EPS = 1e-5
GN_EPS = 64e-5
LANES = 128
MXU_N = 256
VMEM_LIMIT = 56 * 1024 * 1024
Q_SCALE = HEAD_QK ** -0.5 * math.log2(math.e)


def _sigmoid(x):
    return 1.0 / (1.0 + jnp.exp(-x))


def _softplus(x):
    return jnp.maximum(x, 0.0) + jnp.log1p(jnp.exp(-jnp.abs(x)))


def _dot(a, b):
    return jnp.dot(a, b, preferred_element_type=F32)


def _dot_nt(a, b):
    return lax.dot_general(a, b, (((1,), (1,)), ((), ())), preferred_element_type=F32)


def _dot_tn(a, b):
    return lax.dot_general(a, b, (((0,), (0,)), ((), ())), preferred_element_type=F32)


def _inproj_body(x_ref, shift0_ref, npre_ref, mu_ref, cos_ref, s1_ref, s2_ref, w_ref,
                 q_ref, k_ref, kb_ref, v_ref, ga_ref, m_ref, gb_ref, shift_ref, h_scr, carry_scr,
                 *, tiles_per_batch, tm, width, shift_w, col_chunk, k_transposed):
    i = pl.program_id(0)

    @pl.when(i == 0)
    def _():
        carry_scr[...] = jnp.zeros_like(carry_scr)

    x = x_ref[...]
    y = x * lax.rsqrt(jnp.mean(x * x, axis=-1, keepdims=True) + NORM_EPS)
    h_scr[...] = (y * npre_ref[...]).astype(BF16)

    def proj(c0, cw):
        return _dot(h_scr[...], w_ref[:, c0:c0 + cw])

    cos = cos_ref[...]
    s1 = s1_ref[...]
    s2 = s2_ref[...]

    def rope(p):
        outs = []
        for g in range(p.shape[1] // LANES):
            xg = p[:, g * LANES:(g + 1) * LANES]
            outs.append(xg * cos + pltpu.roll(xg, ROT_DIM // 2, axis=1) * s1
                        + pltpu.roll(xg, LANES - ROT_DIM // 2, axis=1) * s2)
        return jnp.concatenate(outs, axis=1)

    n_sub = width // col_chunk
    a_q, a_k, a_v, a_g = 0, width, 2 * width, 3 * width
    b_s = 4 * width
    b_g = b_s + shift_w
    for c in range(n_sub):
        o = c * col_chunk
        sl = slice(o, o + col_chunk)
        q_ref[:, sl] = (rope(proj(a_q + o, col_chunk)) * Q_SCALE).astype(BF16)
        k = rope(proj(a_k + o, col_chunk))
        kb_ref[:, sl] = k.astype(BF16)
        if k_transposed:
            k_ref[0, sl, :] = k.T
        else:
            k_ref[:, sl] = k
        v_ref[:, sl] = proj(a_v + o, col_chunk)
        g = proj(a_g + o, col_chunk)
        ga_ref[:, sl] = (g * _sigmoid(g)).astype(BF16)
        g = proj(b_g + o, col_chunk)
        gb_ref[:, sl] = (g * _sigmoid(g)).astype(BF16)

    is_first = (i % tiles_per_batch) == 0
    o = 0
    while o < shift_w:
        cw = min(col_chunk, shift_w - o)
        sl = slice(o, o + cw)
        ps = proj(b_s + o, cw)
        prev_row = jnp.where(is_first, shift0_ref[0, :, sl], carry_scr[:, sl])
        rows = lax.broadcasted_iota(jnp.int32, ps.shape, 0)
        prev = jnp.where(rows == 0, prev_row, pltpu.roll(ps, 1, axis=0))
        last = ps[tm - 1:tm, :]
        carry_scr[:, sl] = last
        shift_ref[0, :, sl] = last
        m_ref[:, sl] = (ps + (prev - ps) * mu_ref[:, sl]).astype(BF16)
        o += cw


def _inproj(x2, shift0, npre, mu, cos, s1, s2, w_bf, *, batch, seq, tm):
    rows, d = x2.shape
    width = (w_bf.shape[1] - 2 * LORA) // 8
    shift_w = 3 * width + 2 * LORA
    tpb = seq // tm
    k_transposed = tm % LANES == 0
    body = functools.partial(_inproj_body, tiles_per_batch=tpb, tm=tm, width=width,
                             shift_w=shift_w, col_chunk=512, k_transposed=k_transposed)
    if k_transposed:
        k_spec = pl.BlockSpec((1, width, tm), lambda i: (i // tpb, 0, i % tpb))
        k_shape = jax.ShapeDtypeStruct((batch, width, seq), F32)
    else:
        k_spec = pl.BlockSpec((tm, width), lambda i: (i, 0))
        k_shape = jax.ShapeDtypeStruct((rows, width), F32)
    row_blk = lambda w: pl.BlockSpec((tm, w), lambda i: (i, 0))
    full = lambda a: pl.BlockSpec(a.shape, lambda i: (0,) * a.ndim)
    tab = pl.BlockSpec((tm, LANES), lambda i: (i % tpb, 0))
    per_b = pl.BlockSpec((1, 1, shift_w), lambda i: (i // tpb, 0, 0))
    return pl.pallas_call(
        body,
        grid=(rows // tm,),
        in_specs=[row_blk(d), per_b, full(npre), full(mu), tab, tab, tab,
                  pl.BlockSpec(memory_space=pltpu.VMEM)],
        out_specs=[row_blk(width), k_spec, row_blk(width), row_blk(width), row_blk(width),
                   row_blk(shift_w), row_blk(width), per_b],
        out_shape=[jax.ShapeDtypeStruct((rows, width), BF16),
                   k_shape,
                   jax.ShapeDtypeStruct((rows, width), BF16),
                   jax.ShapeDtypeStruct((rows, width), F32),
                   jax.ShapeDtypeStruct((rows, width), BF16),
                   jax.ShapeDtypeStruct((rows, shift_w), BF16),
                   jax.ShapeDtypeStruct((rows, width), BF16),
                   jax.ShapeDtypeStruct((batch, 1, shift_w), F32)],
        scratch_shapes=[pltpu.VMEM((tm, d), BF16), pltpu.VMEM((1, shift_w), F32)],
        compiler_params=pltpu.CompilerParams(dimension_semantics=("arbitrary",),
                                             vmem_limit_bytes=VMEM_LIMIT),
        name="inproj",
    )(x2, shift0, npre, mu, cos, s1, s2, w_bf)


def _attn_body(lam_ref, subln_ref, q_ref, k_ref, v_ref, ga_ref, ya_ref,
               vt_scr, qt_scr, sa_scr, sb_scr, cma_scr, cmb_scr, m_scr, acc_scr,
               *, tq, tk, qb, nq, q_off, lambda_init):
    qi = pl.program_id(2)
    n_kv = vt_scr.shape[0]
    n_blk = 2 * tq // qb
    n_diag = max(tq // tk, 1)
    v_rows = vt_scr.shape[1]

    @pl.when(qi == 0)
    def _():
        def cvt(c, carry):
            sl = pl.ds(pl.multiple_of(c * tk, tk), tk)
            vt_scr[c, 0:HEAD_V, :] = v_ref[0, sl, :].T.astype(BF16)
            vt_scr[c, HEAD_V:, :] = jnp.ones((v_rows - HEAD_V, tk), BF16)
            return carry
        lax.fori_loop(0, n_kv, cvt, 0)

    q = q_ref[...].astype(F32)
    lane = lax.broadcasted_iota(jnp.int32, q.shape, 1)
    qs = jnp.concatenate([jnp.where(lane < HEAD_QK, q, 0.0),
                          jnp.where(lane >= HEAD_QK, q, 0.0)], axis=0)
    for c in range(n_blk):
        qt_scr[c] = qs[c * qb:(c + 1) * qb].T.astype(BF16)

    m_scr[...] = jnp.full_like(m_scr, -jnp.inf)
    acc_scr[...] = jnp.zeros_like(acc_scr)

    def diag_kind(t, c):
        q_lo = (c * qb) % tq
        if qb <= tq and q_lo + qb <= t * tk:
            return "skip"
        if qb <= tq and q_off == 0 and q_lo >= (t + 1) * tk:
            return None
        qrel = (c * qb + lax.broadcasted_iota(jnp.int32, (tk, qb), 1)) % tq
        krel = t * tk + lax.broadcasted_iota(jnp.int32, (tk, qb), 0)
        if q_off:
            return krel // CHUNK <= (q_off + qrel) // CHUNK
        return krel // CHUNK <= qrel // CHUNK

    bufs = ((sa_scr, cma_scr), (sb_scr, cmb_scr))

    def scores(j, buf, kinds):
        s_ref, cm_ref = bufs[buf]
        k_j = k_ref[0, pl.ds(pl.multiple_of(j * tk, tk), tk), :]
        for c in range(n_blk):
            if isinstance(kinds[c], str):
                continue
            s = _dot(k_j, qt_scr[c])
            s_ref[c] = s
            if kinds[c] is None:
                cm_ref[c] = jnp.max(s, axis=0, keepdims=True)

    def consume(j, buf, kinds):
        s_ref, cm_ref = bufs[buf]
        vt_j = vt_scr[j]
        for c in range(n_blk):
            if isinstance(kinds[c], str):
                continue
            s = s_ref[c]
            if kinds[c] is None:
                cm = cm_ref[c]
            else:
                s = jnp.where(kinds[c], s, -jnp.inf)
                cm = jnp.max(s, axis=0, keepdims=True)
            m_prev = m_scr[c]
            m_new = jnp.maximum(m_prev, cm)
            alpha = jnp.exp2(m_prev - m_new)
            p = jnp.exp2(s - m_new).astype(BF16)
            acc_scr[c] = alpha * acc_scr[c] + _dot(vt_j, p)
            m_scr[c] = m_new

    all_visible = [None] * n_blk
    diag = [[diag_kind(t, c) for c in range(n_blk)] for t in range(n_diag)]
    assert not any(isinstance(k, str) for k in diag[0])
    n_full = (qi * tq) // tk
    scores(0, 0, diag[0] if nq == 1 else all_visible)

    def tile_pair(j):
        scores(j + 1, 1, all_visible)
        consume(j, 0, all_visible)
        scores(j + 2, 0, all_visible)
        consume(j + 1, 1, all_visible)

    def quad_step(i, carry):
        tile_pair(4 * i)
        tile_pair(4 * i + 2)
        return carry

    n_pairs = n_full // 2
    if nq > 1:
        lax.fori_loop(0, n_pairs // 2, quad_step, 0)
        if (tq // tk) % 4:
            @pl.when(n_pairs % 2 == 1)
            def _():
                tile_pair(n_full - 2)

    for t in range(n_diag):
        if t + 1 < n_diag:
            scores(n_full + t + 1, (t + 1) % 2, diag[t + 1])
        consume(n_full + t, t % 2, diag[t])

    lam4 = lam_ref[...]
    e1 = jnp.exp(jnp.sum(lam4[0:1] * lam4[1:2], axis=1, keepdims=True))
    e2 = jnp.exp(jnp.sum(lam4[2:3] * lam4[3:4], axis=1, keepdims=True))
    lam = e1 - e2 + lambda_init
    acc = jnp.concatenate([acc_scr[c] for c in range(n_blk)], axis=1)
    o_all = acc[0:HEAD_V] / acc[HEAD_V:HEAD_V + 1]
    o = (o_all[:, :tq] - lam * o_all[:, tq:]).T
    y = o * lax.rsqrt(jnp.mean(o * o, axis=-1, keepdims=True) + SUBLN_EPS)
    y = (y * subln_ref[...]) * (1.0 - lambda_init)
    ya_ref[...] = (y * ga_ref[...].astype(F32)).astype(BF16)


def _attention(lam4, subln, q2, k3, v3, ga2, *, batch, tq_len, tq, tk, q_off, lambda_init):
    tk_len = k3.shape[1]
    heads = k3.shape[2] // HEAD_V
    nq = tq_len // tq
    n_kv = tk_len // tk
    qb = min(MXU_N, 2 * tq)
    n_blk = 2 * tq // qb
    v_rows = HEAD_V + 16
    if q_off:
        assert nq == 1 and tk == tk_len and q_off % CHUNK == 0
    else:
        assert tq % tk == 0 and tq_len == tk_len and tq % CHUNK == 0
        assert nq == 1 or (tq // tk) % 2 == 0
    body = functools.partial(_attn_body, tq=tq, tk=tk, qb=qb, nq=nq, q_off=q_off,
                             lambda_init=lambda_init)
    qblk = pl.BlockSpec((tq, HEAD_V), lambda b, h, i: (b * nq + i, h))
    kvblk = pl.BlockSpec((1, tk_len, HEAD_V), lambda b, h, i: (b, 0, h))
    full = lambda a: pl.BlockSpec(a.shape, lambda b, h, i: (0,) * a.ndim)
    return pl.pallas_call(
        body,
        grid=(batch, heads, nq),
        in_specs=[full(lam4), full(subln), qblk, kvblk, kvblk, qblk],
        out_specs=qblk,
        out_shape=jax.ShapeDtypeStruct(q2.shape, BF16),
        scratch_shapes=[pltpu.VMEM((n_kv, v_rows, tk), BF16),
                        pltpu.VMEM((n_blk, HEAD_V, qb), BF16),
                        pltpu.VMEM((n_blk, tk, qb), F32),
                        pltpu.VMEM((n_blk, tk, qb), F32),
                        pltpu.VMEM((n_blk, 1, qb), F32),
                        pltpu.VMEM((n_blk, 1, qb), F32),
                        pltpu.VMEM((n_blk, 1, qb), F32),
                        pltpu.VMEM((n_blk, v_rows, qb), F32)],
        compiler_params=pltpu.CompilerParams(
            dimension_semantics=("arbitrary", "arbitrary", "arbitrary"),
            vmem_limit_bytes=VMEM_LIMIT),
        name="diffattn",
    )(lam4, subln, q2, k3, v3, ga2)


def _rwkv_body(m_ref, gb_ref, s0_ref, wcomb_ref, w0_ref, a0_ref, kk_ref, ka_ref, rk_ref,
               lnw_ref, lnb_ref, yb_ref, sout_ref, s_scr, *, width, n_sub):
    i = pl.program_id(1)
    L = RW_CHUNK
    gw = MXU_N
    hpg = gw // RW_HEAD
    n_grp = width // gw
    rng = range(n_grp)
    grp = [slice(g * gw, (g + 1) * gw) for g in rng]

    @pl.when(i == 0)
    def _():
        s_scr[...] = s0_ref[0]

    gi = lax.broadcasted_iota(jnp.int32, (gw, gw), 0) // RW_HEAD
    gj = lax.broadcasted_iota(jnp.int32, (gw, gw), 1) // RW_HEAD
    same_head = gi == gj
    group_ones = same_head.astype(BF16)

    def head_sums(*ts):
        rows = jnp.concatenate([t[:, s].astype(BF16) for t in ts for s in grp], axis=0)
        out = _dot(rows, group_ones)
        return [jnp.concatenate([out[(k * n_grp + g) * L:(k * n_grp + g + 1) * L] for g in rng],
                                axis=1) for k in range(len(ts))]

    lane_head = lax.broadcasted_iota(jnp.int32, (L, gw), 1) // RW_HEAD

    def stack(t):
        tb = t.astype(BF16)
        zero = jnp.zeros_like(tb)
        return jnp.concatenate([jnp.where(lane_head == h, tb, zero) for h in range(hpg)], axis=0)

    def prepare(sub, out):
        rows = slice(sub * L, (sub + 1) * L)
        r = m_ref[rows, 0:width].astype(F32)
        kb = m_ref[rows, width:2 * width].astype(F32)
        vb = m_ref[rows, 2 * width:3 * width]
        z = m_ref[rows, 3 * width:3 * width + 2 * LORA].astype(F32)
        lane = lax.broadcasted_iota(jnp.int32, (L, 2 * LORA), 1)
        zt = jnp.where(lane < LORA, jnp.tanh(z), z).astype(BF16)
        lin = _dot(zt, wcomb_ref[...])
        yield
        w_log = -_softplus(-(w0_ref[...] + lin[:, :width])) - 0.5
        logw = -jnp.exp(w_log)
        alpha = _sigmoid(a0_ref[...] + lin[:, width:])

        kk = kb * kk_ref[...]
        kb = kb * (1.0 + (alpha - 1.0) * ka_ref[...])
        kk_sq, bonus_dot = head_sums(kk * kk, r * kb * rk_ref[...])
        yield
        kk = kk / jnp.maximum(jnp.sqrt(kk_sq), 1e-12)

        ti = lax.broadcasted_iota(jnp.int32, (L, L), 0)
        tj = lax.broadcasted_iota(jnp.int32, (L, L), 1)
        tri = (ti >= tj).astype(BF16)
        lw_hi = logw.astype(BF16)
        lw_lo = (logw - lw_hi.astype(F32)).astype(BF16)
        lp = _dot(tri, lw_hi) + _dot(tri, lw_lo)
        yield
        p_inv = jnp.exp(-lp)
        r_hat = r * jnp.exp(lp)
        a_hat = -kk * jnp.exp(lp - logw)
        b_til = kk * alpha * p_inv
        k_til = kb * p_inv
        out.update(
            ar=[jnp.concatenate([a_hat[:, s], r_hat[:, s]], axis=0).astype(BF16) for s in grp],
            bk=[jnp.concatenate([stack(b_til[:, s]), stack(k_til[:, s])], axis=0) for s in grp],
            vs=[stack(vb[:, s]) for s in grp],
            bkr=[jnp.concatenate([b_til[:, s], k_til[:, s]], axis=0).astype(BF16) for s in grp],
            vb=vb, p_last=jnp.exp(lp[L - 1:L, :]), bonus=bonus_dot * vb.astype(F32))

    row = lax.broadcasted_iota(jnp.int32, (L, gw), 0)
    col = lax.broadcasted_iota(jnp.int32, (L, gw), 1) % L
    strict = col < row
    incl = col <= row
    eye = (col == row).astype(F32)

    def solve(pre, s_old, out):
        ar, vs = pre["ar"], pre["vs"]
        gram = [_dot_nt(ar[g], pre["bk"][g]) for g in rng]
        yield
        a_ab = [jnp.where(strict, gm[:L, :gw], 0.0) for gm in gram]
        a_ak = [jnp.where(strict, gm[:L, gw:], 0.0).astype(BF16) for gm in gram]
        a_r = [jnp.concatenate([jnp.where(incl, gm[L:, :gw], 0.0),
                                jnp.where(incl, gm[L:, gw:], 0.0)], axis=1).astype(BF16)
               for gm in gram]

        t_inv = [eye + a for a in a_ab]
        pw = [_dot(a.astype(BF16), stack(a)) for a in a_ab]
        yield
        for _ in range(int(math.log2(L)) - 2):
            res = [_dot(jnp.concatenate([pw[g], t_inv[g]], axis=0).astype(BF16), stack(pw[g]))
                   for g in rng]
            yield
            t_inv = [t_inv[g] + res[g][L:] for g in rng]
            pw = [res[g][:L] for g in rng]
        t_inv = [(t_inv[g] + _dot(t_inv[g].astype(BF16), stack(pw[g]))).astype(BF16) for g in rng]
        yield

        asr = [_dot_nt(ar[g], s_old[g].astype(BF16)) for g in rng]
        yield
        x = [asr[g][:L] + _dot(a_ak[g], vs[g]) for g in rng]
        yield
        u = [_dot(t_inv[g], stack(x[g])) for g in rng]
        yield
        ys = [asr[g][L:] + _dot(a_r[g], jnp.concatenate([stack(u[g]), vs[g]], axis=0))
              for g in rng]
        yield
        s_new = []
        for g, s in enumerate(grp):
            upd = _dot_tn(jnp.concatenate([u[g].astype(BF16), pre["vb"][:, s]], axis=0),
                          pre["bkr"][g])
            s_new.append((s_old[g] + jnp.where(same_head, upd, 0.0)) * pre["p_last"][:, s])
        out.update(s_new=s_new, yb=jnp.concatenate(ys, axis=1))

    def finish(sub, pre, yb):
        inv_n = 1.0 / RW_HEAD
        mean = head_sums(yb)[0] * inv_n
        yield
        d = yb - mean
        var = head_sums(d * d)[0] * inv_n
        yield
        yn = d * lax.rsqrt(var + GN_EPS) * lnw_ref[...] + lnb_ref[...]
        rows = slice(sub * L, (sub + 1) * L)
        yb_ref[rows, :] = ((yn + pre["bonus"]) * gb_ref[rows, :].astype(F32)).astype(BF16)

    def run(main, side=None):
        for _ in main:
            if side is not None and next(side, "done") == "done":
                side = None
        for _ in side or ():
            pass

    state = [s_scr[g] for g in rng]
    pre, res = [dict() for _ in range(n_sub)], [dict() for _ in range(n_sub)]
    run(prepare(0, pre[0]))
    for sub in range(n_sub):
        side = prepare(sub + 1, pre[sub + 1]) if sub + 1 < n_sub else None
        if sub > 0:
            side = itertools.chain(finish(sub - 1, pre[sub - 1], res[sub - 1]["yb"]), side or ())
        run(solve(pre[sub], state, res[sub]), side)
        state = res[sub]["s_new"]
    run(finish(n_sub - 1, pre[n_sub - 1], res[n_sub - 1]["yb"]))
    for g in rng:
        s_scr[g] = state[g]

    @pl.when(i == pl.num_programs(1) - 1)
    def _():
        sout_ref[0] = s_scr[...]


def _rwkv(m2, gb2, s0g, wcomb, w0, a0, k_k, k_a, r_k, ln_w, ln_b, *, batch, seq):
    rows, shift_w = m2.shape
    width = gb2.shape[1]
    nc = seq // RW_CHUNK
    n_sub = 2 if nc % 2 == 0 else 1
    n_steps = nc // n_sub
    n_grp = width // MXU_N
    body = functools.partial(_rwkv_body, width=width, n_sub=n_sub)
    row_blk = lambda w: pl.BlockSpec((n_sub * RW_CHUNK, w), lambda b, i: (b * n_steps + i, 0))
    full = lambda a: pl.BlockSpec(a.shape, lambda b, i: (0,) * a.ndim)
    st = pl.BlockSpec((1, n_grp, MXU_N, MXU_N), lambda b, i: (b, 0, 0, 0))
    return pl.pallas_call(
        body,
        grid=(batch, n_steps),
        in_specs=[row_blk(shift_w), row_blk(width), st, full(wcomb), full(w0), full(a0),
                  full(k_k), full(k_a), full(r_k), full(ln_w), full(ln_b)],
        out_specs=[row_blk(width), st],
        out_shape=[jax.ShapeDtypeStruct((rows, width), BF16),
                   jax.ShapeDtypeStruct(s0g.shape, F32)],
        scratch_shapes=[pltpu.VMEM((n_grp, MXU_N, MXU_N), F32)],
        compiler_params=pltpu.CompilerParams(dimension_semantics=("arbitrary", "arbitrary"),
                                             vmem_limit_bytes=VMEM_LIMIT),
        name="rwkv7",
    )(m2, gb2, s0g, wcomb, w0, a0, k_k, k_a, r_k, ln_w, ln_b)


def _outproj_body(x_ref, ya_ref, yb_ref, wa_ref, wb_ref, npost_ref, y_ref):
    out = _dot(ya_ref[...], wa_ref[...]) + _dot(yb_ref[...], wb_ref[...])
    y = out * lax.rsqrt(jnp.mean(out * out, axis=-1, keepdims=True) + NORM_EPS)
    y_ref[...] = x_ref[...] + y * npost_ref[...]


def _outproj(x2, ya2, yb2, wa, wb, npost, *, tm):
    rows, d = x2.shape
    row_blk = lambda w: pl.BlockSpec((tm, w), lambda i: (i, 0))
    full = lambda a: pl.BlockSpec(a.shape, lambda i: (0,) * a.ndim)
    return pl.pallas_call(
        _outproj_body,
        grid=(rows // tm,),
        in_specs=[row_blk(d), row_blk(ya2.shape[1]), row_blk(yb2.shape[1]),
                  full(wa), full(wb), full(npost)],
        out_specs=row_blk(d),
        out_shape=jax.ShapeDtypeStruct((rows, d), F32),
        compiler_params=pltpu.CompilerParams(dimension_semantics=("arbitrary",),
                                             vmem_limit_bytes=VMEM_LIMIT),
        name="outproj",
    )(x2, ya2, yb2, wa, wb, npost)


def _rope_tables(pos):
    half = ROT_DIM // 2
    inv = jnp.power(jnp.float32(ROPE_THETA), -jnp.arange(half, dtype=F32) * (2.0 / ROT_DIM))
    ang = pos.astype(F32)[:, None] * inv[None, :]
    cos, sin = jnp.cos(ang), jnp.sin(ang)
    n = pos.shape[0]
    pad = HEAD_QK - ROT_DIM
    zeros, ones = jnp.zeros((n, half), F32), jnp.ones((n, pad), F32)
    zpad = jnp.zeros((n, pad), F32)
    c = jnp.concatenate([cos, cos, ones], axis=1)
    s_prev = jnp.concatenate([zeros, sin, zpad], axis=1)
    s_next = jnp.concatenate([-sin, zeros, zpad], axis=1)
    rep = LANES // HEAD_QK
    return tuple(jnp.tile(t, (1, rep)) for t in (c, s_prev, s_next))


def _group_states(s):
    b, h, n, _ = s.shape
    hpg = MXU_N // n
    s = s.reshape(b, h // hpg, hpg, n, n)
    out = jnp.einsum("bghij,hk->bghikj", s, jnp.eye(hpg, dtype=s.dtype))
    return out.reshape(b, h // hpg, MXU_N, MXU_N)


def _ungroup_states(sg):
    n = RW_HEAD
    hpg = MXU_N // n
    b, g = sg.shape[:2]
    s = jnp.einsum("bghikj,hk->bghij", sg.reshape(b, g, hpg, n, hpg, n), jnp.eye(hpg, dtype=sg.dtype))
    return s.reshape(b, g * hpg, n, n)


def _stream(x, pos, past_k, past_v, wkv0, shift0, p, lambda_init):
    batch, seq, d = x.shape
    width = p["w_out_a"].shape[0]
    heads = width // HEAD_V
    rows = batch * seq
    x2 = x.reshape(rows, d)
    tm = min(256, seq)
    cos, s1, s2 = _rope_tables(pos)
    q2, k_out, kb2, v2, ga2, m2, gb2, shift_out = _inproj(
        x2, shift0, p["norm_pre"], p["mu_shift"], cos, s1, s2, p["w_in"],
        batch=batch, seq=seq, tm=tm)

    if k_out.shape[0] == batch:
        k_out = k_out.reshape(batch, heads, 2, HEAD_QK, seq).transpose(0, 4, 1, 2, 3)
    else:
        k_out = k_out.reshape(batch, seq, heads, 2, HEAD_QK)
    k3 = kb2.reshape(batch, seq, width)
    v3 = v2.reshape(batch, seq, width)
    if past_k is None:
        k_all, v_all, q_off = k3, v3, 0
        tq, tk = min(1024, seq), min(256, seq)
    else:
        past = past_k.shape[1]
        total = past + seq
        padded = -(-total // LANES) * LANES
        zpad = jnp.zeros((batch, padded - total, width), F32)
        k_all = jnp.concatenate([past_k.reshape(batch, past, width).astype(BF16), k3,
                                 zpad.astype(BF16)], axis=1)
        v_all = jnp.concatenate([past_v.reshape(batch, past, width), v3, zpad], axis=1)
        q_off, tq, tk = past, seq, padded
    ya2 = _attention(p["lam4"], p["subln"], q2, k_all, v_all, ga2, batch=batch, tq_len=seq,
                     tq=tq, tk=tk, q_off=q_off, lambda_init=lambda_init)

    yb2, s_groups = _rwkv(m2, gb2, _group_states(wkv0), p["w_comb"], p["w0"], p["a0"], p["k_k"],
                          p["k_a"], p["r_k"], p["ln_x_w"], p["ln_x_b"], batch=batch, seq=seq)

    y2 = _outproj(x2, ya2, yb2, p["w_out_a"], p["w_out_b"], p["norm_post"], tm=tm)
    return (y2.reshape(batch, seq, d),
            k_out,
            v3.reshape(batch, seq, heads, HEAD_V),
            _ungroup_states(s_groups),
            shift_out)


def kernel(x_prompt, x_sample, cache_k, cache_v, state_wkv, state_shift, norm_pre, w_in, lam_q1, lam_k1, lam_q2, lam_k2, subln, mu_shift, w0, w_up, a0, a_up, k_k, k_a, r_k, ln_x_w, ln_x_b, w_out, norm_post):
    depth = w_in.shape[0]
    assert depth == 1, "single-layer problem"
    l = 0
    lambda_init = 0.8 - 0.6 * math.exp(-0.3 * l)
    width = w_up.shape[2]
    row = lambda a: a[l].reshape(1, -1).astype(F32)
    zeros = jnp.zeros((LORA, width), F32)
    p = {
        "norm_pre": row(norm_pre), "norm_post": row(norm_post), "subln": row(subln),
        "mu_shift": row(mu_shift), "w0": row(w0), "a0": row(a0), "k_k": row(k_k), "k_a": row(k_a),
        "r_k": row(r_k), "ln_x_w": row(ln_x_w), "ln_x_b": row(ln_x_b),
        "w_in": w_in[l].astype(BF16),
        "w_out_a": w_out[l, :width].astype(BF16), "w_out_b": w_out[l, width:].astype(BF16),
        "w_comb": jnp.concatenate([jnp.concatenate([w_up[l], zeros], axis=1),
                                   jnp.concatenate([zeros, a_up[l]], axis=1)], axis=0).astype(BF16),
        "lam4": jnp.stack([lam_q1[l], lam_k1[l], lam_q2[l], lam_k2[l]]).astype(F32),
    }
    b_p, t_p, _ = x_prompt.shape
    b_s, t_s, _ = x_sample.shape
    past_len = cache_k.shape[2]
    shift_w = state_shift.shape[-1]
    n_heads_b = state_wkv.shape[2]
    pos_p = jnp.arange(t_p, dtype=jnp.int32)
    pos_s = past_len + jnp.arange(t_s, dtype=jnp.int32)

    yp, kp, vp, wp, sp = _stream(
        x_prompt, pos_p, None, None,
        jnp.zeros((b_p, n_heads_b, RW_HEAD, RW_HEAD), F32), jnp.zeros((b_p, 1, shift_w), F32),
        p, lambda_init)
    ys, ks, vs, ws, ss = _stream(
        x_sample, pos_s, cache_k[l], cache_v[l], state_wkv[l].astype(F32), state_shift[l],
        p, lambda_init)
    return (yp, ys, kp[None], vp[None], wp[None], sp[None],
            ks[None], vs[None], ws[None].astype(state_wkv.dtype), ss[None])
```

```python
import functools
import itertools
import math

import jax
import jax.numpy as jnp
from jax import lax
from jax.experimental import pallas as pl
from jax.experimental.pallas import tpu as pltpu

F32 = jnp.float32
BF16 = jnp.bfloat16

CHUNK = 64
HEAD_V = 128
HEAD_QK = 64
ROT_DIM = 16
ROPE_THETA = 500000.0
RW_HEAD = 64
RW_CHUNK = 64
LORA = 64
NORM_EPS = 1e-6
SUBLN_EPS = 1e-5
GN_EPS = 64e-5
LANES = 128
MXU_N = 256
VMEM_LIMIT = 56 * 1024 * 1024
Q_SCALE = HEAD_QK ** -0.5 * math.log2(math.e)


def _sigmoid(x):
    return 1.0 / (1.0 + jnp.exp(-x))


def _softplus(x):
    return jnp.maximum(x, 0.0) + jnp.log1p(jnp.exp(-jnp.abs(x)))


def _dot(a, b):
    return jnp.dot(a, b, preferred_element_type=F32)


def _dot_nt(a, b):
    return lax.dot_general(a, b, (((1,), (1,)), ((), ())), preferred_element_type=F32)


def _dot_tn(a, b):
    return lax.dot_general(a, b, (((0,), (0,)), ((), ())), preferred_element_type=F32)


def _inproj_body(x_ref, shift0_ref, npre_ref, mu_ref, cos_ref, s1_ref, s2_ref, w_ref,
                 q_ref, k_ref, kb_ref, v_ref, ga_ref, m_ref, gb_ref, shift_ref, h_scr, carry_scr,
                 *, tiles_per_batch, tm, width, shift_w, col_chunk, k_transposed):
    i = pl.program_id(0)

    @pl.when(i == 0)
    def _():
        carry_scr[...] = jnp.zeros_like(carry_scr)

    x = x_ref[...]
    y = x * lax.rsqrt(jnp.mean(x * x, axis=-1, keepdims=True) + NORM_EPS)
    h_scr[...] = (y * npre_ref[...]).astype(BF16)

    def proj(c0, cw):
        return _dot(h_scr[...], w_ref[:, c0:c0 + cw])

    cos = cos_ref[...]
    s1 = s1_ref[...]
    s2 = s2_ref[...]

    def rope(p):
        outs = []
        for g in range(p.shape[1] // LANES):
            xg = p[:, g * LANES:(g + 1) * LANES]
            outs.append(xg * cos + pltpu.roll(xg, ROT_DIM // 2, axis=1) * s1
                        + pltpu.roll(xg, LANES - ROT_DIM // 2, axis=1) * s2)
        return jnp.concatenate(outs, axis=1)

    n_sub = width // col_chunk
    a_q, a_k, a_v, a_g = 0, width, 2 * width, 3 * width
    b_s = 4 * width
    b_g = b_s + shift_w
    for c in range(n_sub):
        o = c * col_chunk
        sl = slice(o, o + col_chunk)
        q_ref[:, sl] = (rope(proj(a_q + o, col_chunk)) * Q_SCALE).astype(BF16)
        k = rope(proj(a_k + o, col_chunk))
        kb_ref[:, sl] = k.astype(BF16)
        if k_transposed:
            k_ref[0, sl, :] = k.T
        else:
            k_ref[:, sl] = k
        v_ref[:, sl] = proj(a_v + o, col_chunk)
        g = proj(a_g + o, col_chunk)
        ga_ref[:, sl] = (g * _sigmoid(g)).astype(BF16)
        g = proj(b_g + o, col_chunk)
        gb_ref[:, sl] = (g * _sigmoid(g)).astype(BF16)

    is_first = (i % tiles_per_batch) == 0
    o = 0
    while o < shift_w:
        cw = min(col_chunk, shift_w - o)
        sl = slice(o, o + cw)
        ps = proj(b_s + o, cw)
        prev_row = jnp.where(is_first, shift0_ref[0, :, sl], carry_scr[:, sl])
        rows = lax.broadcasted_iota(jnp.int32, ps.shape, 0)
        prev = jnp.where(rows == 0, prev_row, pltpu.roll(ps, 1, axis=0))
        last = ps[tm - 1:tm, :]
        carry_scr[:, sl] = last
        shift_ref[0, :, sl] = last
        m_ref[:, sl] = (ps + (prev - ps) * mu_ref[:, sl]).astype(BF16)
        o += cw


def _inproj(x2, shift0, npre, mu, cos, s1, s2, w_bf, *, batch, seq, tm):
    rows, d = x2.shape
    width = (w_bf.shape[1] - 2 * LORA) // 8
    shift_w = 3 * width + 2 * LORA
    tpb = seq // tm
    k_transposed = tm % LANES == 0
    body = functools.partial(_inproj_body, tiles_per_batch=tpb, tm=tm, width=width,
                             shift_w=shift_w, col_chunk=512, k_transposed=k_transposed)
    if k_transposed:
        k_spec = pl.BlockSpec((1, width, tm), lambda i: (i // tpb, 0, i % tpb))
        k_shape = jax.ShapeDtypeStruct((batch, width, seq), F32)
    else:
        k_spec = pl.BlockSpec((tm, width), lambda i: (i, 0))
        k_shape = jax.ShapeDtypeStruct((rows, width), F32)
    row_blk = lambda w: pl.BlockSpec((tm, w), lambda i: (i, 0))
    full = lambda a: pl.BlockSpec(a.shape, lambda i: (0,) * a.ndim)
    tab = pl.BlockSpec((tm, LANES), lambda i: (i % tpb, 0))
    per_b = pl.BlockSpec((1, 1, shift_w), lambda i: (i // tpb, 0, 0))
    return pl.pallas_call(
        body,
        grid=(rows // tm,),
        in_specs=[row_blk(d), per_b, full(npre), full(mu), tab, tab, tab,
                  pl.BlockSpec(memory_space=pltpu.VMEM)],
        out_specs=[row_blk(width), k_spec, row_blk(width), row_blk(width), row_blk(width),
                   row_blk(shift_w), row_blk(width), per_b],
        out_shape=[jax.ShapeDtypeStruct((rows, width), BF16),
                   k_shape,
                   jax.ShapeDtypeStruct((rows, width), BF16),
                   jax.ShapeDtypeStruct((rows, width), F32),
                   jax.ShapeDtypeStruct((rows, width), BF16),
                   jax.ShapeDtypeStruct((rows, shift_w), BF16),
                   jax.ShapeDtypeStruct((rows, width), BF16),
                   jax.ShapeDtypeStruct((batch, 1, shift_w), F32)],
        scratch_shapes=[pltpu.VMEM((tm, d), BF16), pltpu.VMEM((1, shift_w), F32)],
        compiler_params=pltpu.CompilerParams(dimension_semantics=("arbitrary",),
                                             vmem_limit_bytes=VMEM_LIMIT),
        name="inproj",
    )(x2, shift0, npre, mu, cos, s1, s2, w_bf)


def _attn_body(lam_ref, subln_ref, q_ref, k_ref, v_ref, ga_ref, ya_ref,
               vt_scr, qt_scr, sa_scr, sb_scr, cma_scr, cmb_scr, m_scr, acc_scr,
               *, tq, tk, qb, nq, q_off, lambda_init):
    qi = pl.program_id(2)
    n_kv = vt_scr.shape[0]
    n_blk = 2 * tq // qb
    n_diag = max(tq // tk, 1)
    v_rows = vt_scr.shape[1]

    @pl.when(qi == 0)
    def _():
        def cvt(c, carry):
            sl = pl.ds(pl.multiple_of(c * tk, tk), tk)
            vt_scr[c, 0:HEAD_V, :] = v_ref[0, sl, :].T.astype(BF16)
            vt_scr[c, HEAD_V:, :] = jnp.ones((v_rows - HEAD_V, tk), BF16)
            return carry
        lax.fori_loop(0, n_kv, cvt, 0)

    q = q_ref[...].astype(F32)
    lane = lax.broadcasted_iota(jnp.int32, q.shape, 1)
    qs = jnp.concatenate([jnp.where(lane < HEAD_QK, q, 0.0),
                          jnp.where(lane >= HEAD_QK, q, 0.0)], axis=0)
    for c in range(n_blk):
        qt_scr[c] = qs[c * qb:(c + 1) * qb].T.astype(BF16)

    m_scr[...] = jnp.full_like(m_scr, -jnp.inf)
    acc_scr[...] = jnp.zeros_like(acc_scr)

    def diag_kind(t, c):
        q_lo = (c * qb) % tq
        if qb <= tq and q_lo + qb <= t * tk:
            return "skip"
        if qb <= tq and q_off == 0 and q_lo >= (t + 1) * tk:
            return None
        qrel = (c * qb + lax.broadcasted_iota(jnp.int32, (tk, qb), 1)) % tq
        krel = t * tk + lax.broadcasted_iota(jnp.int32, (tk, qb), 0)
        if q_off:
            return krel // CHUNK <= (q_off + qrel) // CHUNK
        return krel // CHUNK <= qrel // CHUNK

    bufs = ((sa_scr, cma_scr), (sb_scr, cmb_scr))

    def scores(j, buf, kinds):
        s_ref, cm_ref = bufs[buf]
        k_j = k_ref[0, pl.ds(pl.multiple_of(j * tk, tk), tk), :]
        for c in range(n_blk):
            if isinstance(kinds[c], str):
                continue
            s = _dot(k_j, qt_scr[c])
            s_ref[c] = s
            if kinds[c] is None:
                cm_ref[c] = jnp.max(s, axis=0, keepdims=True)

    def consume(j, buf, kinds):
        s_ref, cm_ref = bufs[buf]
        vt_j = vt_scr[j]
        for c in range(n_blk):
            if isinstance(kinds[c], str):
                continue
            s = s_ref[c]
            if kinds[c] is None:
                cm = cm_ref[c]
            else:
                s = jnp.where(kinds[c], s, -jnp.inf)
                cm = jnp.max(s, axis=0, keepdims=True)
            m_prev = m_scr[c]
            m_new = jnp.maximum(m_prev, cm)
            alpha = jnp.exp2(m_prev - m_new)
            p = jnp.exp2(s - m_new).astype(BF16)
            acc_scr[c] = alpha * acc_scr[c] + _dot(vt_j, p)
            m_scr[c] = m_new

    all_visible = [None] * n_blk
    diag = [[diag_kind(t, c) for c in range(n_blk)] for t in range(n_diag)]
    assert not any(isinstance(k, str) for k in diag[0])
    n_full = (qi * tq) // tk
    scores(0, 0, diag[0] if nq == 1 else all_visible)

    def tile_pair(j):
        scores(j + 1, 1, all_visible)
        consume(j, 0, all_visible)
        scores(j + 2, 0, all_visible)
        consume(j + 1, 1, all_visible)

    def quad_step(i, carry):
        tile_pair(4 * i)
        tile_pair(4 * i + 2)
        return carry

    n_pairs = n_full // 2
    if nq > 1:
        lax.fori_loop(0, n_pairs // 2, quad_step, 0)
        if (tq // tk) % 4:
            @pl.when(n_pairs % 2 == 1)
            def _():
                tile_pair(n_full - 2)

    for t in range(n_diag):
        if t + 1 < n_diag:
            scores(n_full + t + 1, (t + 1) % 2, diag[t + 1])
        consume(n_full + t, t % 2, diag[t])

    lam4 = lam_ref[...]
    e1 = jnp.exp(jnp.sum(lam4[0:1] * lam4[1:2], axis=1, keepdims=True))
    e2 = jnp.exp(jnp.sum(lam4[2:3] * lam4[3:4], axis=1, keepdims=True))
    lam = e1 - e2 + lambda_init
    acc = jnp.concatenate([acc_scr[c] for c in range(n_blk)], axis=1)
    o_all = acc[0:HEAD_V] * (1.0 / acc[HEAD_V:HEAD_V + 1])
    o = (o_all[:, :tq] - lam * o_all[:, tq:]).T
    y = o * lax.rsqrt(jnp.mean(o * o, axis=-1, keepdims=True) + SUBLN_EPS)
    y = (y * subln_ref[...]) * (1.0 - lambda_init)
    ya_ref[...] = (y * ga_ref[...].astype(F32)).astype(BF16)


def _attention(lam4, subln, q2, k3, v3, ga2, *, batch, tq_len, tq, tk, q_off, lambda_init):
    tk_len = k3.shape[1]
    heads = k3.shape[2] // HEAD_V
    nq = tq_len // tq
    n_kv = tk_len // tk
    qb = min(MXU_N, 2 * tq)
    n_blk = 2 * tq // qb
    v_rows = HEAD_V + 16
    if q_off:
        assert nq == 1 and tk == tk_len and q_off % CHUNK == 0
    else:
        assert tq % tk == 0 and tq_len == tk_len and tq % CHUNK == 0
        assert nq == 1 or (tq // tk) % 2 == 0
    body = functools.partial(_attn_body, tq=tq, tk=tk, qb=qb, nq=nq, q_off=q_off,
                             lambda_init=lambda_init)
    qblk = pl.BlockSpec((tq, HEAD_V), lambda b, h, i: (b * nq + i, h))
    kvblk = pl.BlockSpec((1, tk_len, HEAD_V), lambda b, h, i: (b, 0, h))
    full = lambda a: pl.BlockSpec(a.shape, lambda b, h, i: (0,) * a.ndim)
    return pl.pallas_call(
        body,
        grid=(batch, heads, nq),
        in_specs=[full(lam4), full(subln), qblk, kvblk, kvblk, qblk],
        out_specs=qblk,
        out_shape=jax.ShapeDtypeStruct(q2.shape, BF16),
        scratch_shapes=[pltpu.VMEM((n_kv, v_rows, tk), BF16),
                        pltpu.VMEM((n_blk, HEAD_V, qb), BF16),
                        pltpu.VMEM((n_blk, tk, qb), F32),
                        pltpu.VMEM((n_blk, tk, qb), F32),
                        pltpu.VMEM((n_blk, 1, qb), F32),
                        pltpu.VMEM((n_blk, 1, qb), F32),
                        pltpu.VMEM((n_blk, 1, qb), F32),
                        pltpu.VMEM((n_blk, v_rows, qb), F32)],
        compiler_params=pltpu.CompilerParams(
            dimension_semantics=("arbitrary", "arbitrary", "arbitrary"),
            vmem_limit_bytes=VMEM_LIMIT),
        name="diffattn",
    )(lam4, subln, q2, k3, v3, ga2)


def _rwkv_body(m_ref, gb_ref, s0_ref, wcomb_ref, w0_ref, a0_ref, kk_ref, ka_ref, rk_ref,
               lnw_ref, lnb_ref, yb_ref, sout_ref, s_scr, *, width, n_sub):
    i = pl.program_id(1)
    L = RW_CHUNK
    gw = MXU_N
    hpg = gw // RW_HEAD
    n_grp = width // gw
    rng = range(n_grp)
    grp = [slice(g * gw, (g + 1) * gw) for g in rng]

    @pl.when(i == 0)
    def _():
        s_scr[...] = s0_ref[0]

    gi = lax.broadcasted_iota(jnp.int32, (gw, gw), 0) // RW_HEAD
    gj = lax.broadcasted_iota(jnp.int32, (gw, gw), 1) // RW_HEAD
    same_head = gi == gj
    group_ones = same_head.astype(BF16)

    def head_sums(*ts):
        rows = jnp.concatenate([t[:, s].astype(BF16) for t in ts for s in grp], axis=0)
        out = _dot(rows, group_ones)
        return [jnp.concatenate([out[(k * n_grp + g) * L:(k * n_grp + g + 1) * L] for g in rng],
                                axis=1) for k in range(len(ts))]

    lane_head = lax.broadcasted_iota(jnp.int32, (L, gw), 1) // RW_HEAD

    def stack(t):
        tb = t.astype(BF16)
        zero = jnp.zeros_like(tb)
        return jnp.concatenate([jnp.where(lane_head == h, tb, zero) for h in range(hpg)], axis=0)

    def prepare(sub, out):
        rows = slice(sub * L, (sub + 1) * L)
        r = m_ref[rows, 0:width].astype(F32)
        kb = m_ref[rows, width:2 * width].astype(F32)
        vb = m_ref[rows, 2 * width:3 * width]
        z = m_ref[rows, 3 * width:3 * width + 2 * LORA].astype(F32)
        lane = lax.broadcasted_iota(jnp.int32, (L, 2 * LORA), 1)
        zt = jnp.where(lane < LORA, jnp.tanh(z), z).astype(BF16)
        lin = _dot(zt, wcomb_ref[...])
        yield
        w_log = -_softplus(-(w0_ref[...] + lin[:, :width])) - 0.5
        logw = -jnp.exp(w_log)
        alpha = _sigmoid(a0_ref[...] + lin[:, width:])

        kk = kb * kk_ref[...]
        kb = kb * (1.0 + (alpha - 1.0) * ka_ref[...])
        kk_sq, bonus_dot = head_sums(kk * kk, r * kb * rk_ref[...])
        yield
        kk = kk / jnp.maximum(jnp.sqrt(kk_sq), 1e-12)

        ti = lax.broadcasted_iota(jnp.int32, (L, L), 0)
        tj = lax.broadcasted_iota(jnp.int32, (L, L), 1)
        tri = (ti >= tj).astype(BF16)
        lw_hi = logw.astype(BF16)
        lw_lo = (logw - lw_hi.astype(F32)).astype(BF16)
        lp = _dot(tri, lw_hi) + _dot(tri, lw_lo)
        yield
        p_inv = jnp.exp(-lp)
        r_hat = r * jnp.exp(lp)
        a_hat = -kk * jnp.exp(lp - logw)
        b_til = kk * alpha * p_inv
        k_til = kb * p_inv
        out.update(
            ar=[jnp.concatenate([a_hat[:, s], r_hat[:, s]], axis=0).astype(BF16) for s in grp],
            bk=[jnp.concatenate([stack(b_til[:, s]), stack(k_til[:, s])], axis=0) for s in grp],
            vs=[stack(vb[:, s]) for s in grp],
            bkr=[jnp.concatenate([b_til[:, s], k_til[:, s]], axis=0).astype(BF16) for s in grp],
            vb=vb, p_last=jnp.exp(lp[L - 1:L, :]), bonus=bonus_dot * vb.astype(F32))

    row = lax.broadcasted_iota(jnp.int32, (L, gw), 0)
    col = lax.broadcasted_iota(jnp.int32, (L, gw), 1) % L
    strict = col < row
    incl = col <= row
    eye = (col == row).astype(F32)

    def solve(pre, s_old, out):
        ar, vs = pre["ar"], pre["vs"]
        gram = [_dot_nt(ar[g], pre["bk"][g]) for g in rng]
        yield
        a_ab = [jnp.where(strict, gm[:L, :gw], 0.0) for gm in gram]
        a_ak = [jnp.where(strict, gm[:L, gw:], 0.0).astype(BF16) for gm in gram]
        a_r = [jnp.concatenate([jnp.where(incl, gm[L:, :gw], 0.0),
                                jnp.where(incl, gm[L:, gw:], 0.0)], axis=1).astype(BF16)
               for gm in gram]

        t_inv = [eye + a for a in a_ab]
        pw = [_dot(a.astype(BF16), stack(a)) for a in a_ab]
        yield
        for _ in range(int(math.log2(L)) - 2):
            res = [_dot(jnp.concatenate([pw[g], t_inv[g]], axis=0).astype(BF16), stack(pw[g]))
                   for g in rng]
            yield
            t_inv = [t_inv[g] + res[g][L:] for g in rng]
            pw = [res[g][:L] for g in rng]
        t_inv = [(t_inv[g] + _dot(t_inv[g].astype(BF16), stack(pw[g]))).astype(BF16) for g in rng]
        yield

        asr = [_dot_nt(ar[g], s_old[g].astype(BF16)) for g in rng]
        yield
        x = [asr[g][:L] + _dot(a_ak[g], vs[g]) for g in rng]
        yield
        u = [_dot(t_inv[g], stack(x[g])) for g in rng]
        yield
        ys = [asr[g][L:] + _dot(a_r[g], jnp.concatenate([stack(u[g]), vs[g]], axis=0))
              for g in rng]
        yield
        s_new = []
        for g, s in enumerate(grp):
            upd = _dot_tn(jnp.concatenate([u[g].astype(BF16), pre["vb"][:, s]], axis=0),
                          pre["bkr"][g])
            s_new.append((s_old[g] + jnp.where(same_head, upd, 0.0)) * pre["p_last"][:, s])
        out.update(s_new=s_new, yb=jnp.concatenate(ys, axis=1))

    def finish(sub, pre, yb):
        inv_n = 1.0 / RW_HEAD
        mean = head_sums(yb)[0] * inv_n
        yield
        d = yb - mean
        var = head_sums(d * d)[0] * inv_n
        yield
        yn = d * lax.rsqrt(var + GN_EPS) * lnw_ref[...] + lnb_ref[...]
        rows = slice(sub * L, (sub + 1) * L)
        yb_ref[rows, :] = ((yn + pre["bonus"]) * gb_ref[rows, :].astype(F32)).astype(BF16)

    def run(main, side=None):
        for _ in main:
            if side is not None and next(side, "done") == "done":
                side = None
        for _ in side or ():
            pass

    state = [s_scr[g] for g in rng]
    pre, res = [dict() for _ in range(n_sub)], [dict() for _ in range(n_sub)]
    run(prepare(0, pre[0]))
    for sub in range(n_sub):
        side = prepare(sub + 1, pre[sub + 1]) if sub + 1 < n_sub else None
        if sub > 0:
            side = itertools.chain(finish(sub - 1, pre[sub - 1], res[sub - 1]["yb"]), side or ())
        run(solve(pre[sub], state, res[sub]), side)
        state = res[sub]["s_new"]
    run(finish(n_sub - 1, pre[n_sub - 1], res[n_sub - 1]["yb"]))
    for g in rng:
        s_scr[g] = state[g]

    @pl.when(i == pl.num_programs(1) - 1)
    def _():
        sout_ref[0] = s_scr[...]


def _rwkv(m2, gb2, s0g, wcomb, w0, a0, k_k, k_a, r_k, ln_w, ln_b, *, batch, seq):
    rows, shift_w = m2.shape
    width = gb2.shape[1]
    nc = seq // RW_CHUNK
    n_sub = next(n for n in (8, 4, 2, 1) if nc % n == 0)
    n_steps = nc // n_sub
    n_grp = width // MXU_N
    body = functools.partial(_rwkv_body, width=width, n_sub=n_sub)
    row_blk = lambda w: pl.BlockSpec((n_sub * RW_CHUNK, w), lambda b, i: (b * n_steps + i, 0))
    full = lambda a: pl.BlockSpec(a.shape, lambda b, i: (0,) * a.ndim)
    st = pl.BlockSpec((1, n_grp, MXU_N, MXU_N), lambda b, i: (b, 0, 0, 0))
    return pl.pallas_call(
        body,
        grid=(batch, n_steps),
        in_specs=[row_blk(shift_w), row_blk(width), st, full(wcomb), full(w0), full(a0),
                  full(k_k), full(k_a), full(r_k), full(ln_w), full(ln_b)],
        out_specs=[row_blk(width), st],
        out_shape=[jax.ShapeDtypeStruct((rows, width), BF16),
                   jax.ShapeDtypeStruct(s0g.shape, F32)],
        scratch_shapes=[pltpu.VMEM((n_grp, MXU_N, MXU_N), F32)],
        compiler_params=pltpu.CompilerParams(dimension_semantics=("arbitrary", "arbitrary"),
                                             vmem_limit_bytes=VMEM_LIMIT),
        name="rwkv7",
    )(m2, gb2, s0g, wcomb, w0, a0, k_k, k_a, r_k, ln_w, ln_b)


def _outproj_body(x_ref, ya_ref, yb_ref, wa_ref, wb_ref, npost_ref, y_ref):
    out = _dot(ya_ref[...], wa_ref[...]) + _dot(yb_ref[...], wb_ref[...])
    y = out * lax.rsqrt(jnp.mean(out * out, axis=-1, keepdims=True) + NORM_EPS)
    y_ref[...] = x_ref[...] + y * npost_ref[...]


def _outproj(x2, ya2, yb2, wa, wb, npost, *, tm):
    rows, d = x2.shape
    row_blk = lambda w: pl.BlockSpec((tm, w), lambda i: (i, 0))
    full = lambda a: pl.BlockSpec(a.shape, lambda i: (0,) * a.ndim)
    return pl.pallas_call(
        _outproj_body,
        grid=(rows // tm,),
        in_specs=[row_blk(d), row_blk(ya2.shape[1]), row_blk(yb2.shape[1]),
                  full(wa), full(wb), full(npost)],
        out_specs=row_blk(d),
        out_shape=jax.ShapeDtypeStruct((rows, d), F32),
        compiler_params=pltpu.CompilerParams(dimension_semantics=("arbitrary",),
                                             vmem_limit_bytes=VMEM_LIMIT),
        name="outproj",
    )(x2, ya2, yb2, wa, wb, npost)


def _rope_tables(pos):
    half = ROT_DIM // 2
    inv = jnp.power(jnp.float32(ROPE_THETA), -jnp.arange(half, dtype=F32) * (2.0 / ROT_DIM))
    ang = pos.astype(F32)[:, None] * inv[None, :]
    cos, sin = jnp.cos(ang), jnp.sin(ang)
    n = pos.shape[0]
    pad = HEAD_QK - ROT_DIM
    zeros, ones = jnp.zeros((n, half), F32), jnp.ones((n, pad), F32)
    zpad = jnp.zeros((n, pad), F32)
    c = jnp.concatenate([cos, cos, ones], axis=1)
    s_prev = jnp.concatenate([zeros, sin, zpad], axis=1)
    s_next = jnp.concatenate([-sin, zeros, zpad], axis=1)
    rep = LANES // HEAD_QK
    return tuple(jnp.tile(t, (1, rep)) for t in (c, s_prev, s_next))


def _group_states(s):
    b, h, n, _ = s.shape
    hpg = MXU_N // n
    s = s.reshape(b, h // hpg, hpg, n, n)
    out = jnp.einsum("bghij,hk->bghikj", s, jnp.eye(hpg, dtype=s.dtype))
    return out.reshape(b, h // hpg, MXU_N, MXU_N)


def _ungroup_states(sg):
    n = RW_HEAD
    hpg = MXU_N // n
    b, g = sg.shape[:2]
    s = jnp.einsum("bghikj,hk->bghij", sg.reshape(b, g, hpg, n, hpg, n), jnp.eye(hpg, dtype=sg.dtype))
    return s.reshape(b, g * hpg, n, n)


def _stream(x, pos, past_k, past_v, wkv0, shift0, p, lambda_init):
    batch, seq, d = x.shape
    width = p["w_out_a"].shape[0]
    heads = width // HEAD_V
    rows = batch * seq
    x2 = x.reshape(rows, d)
    tm = min(256, seq)
    cos, s1, s2 = _rope_tables(pos)
    q2, k_out, kb2, v2, ga2, m2, gb2, shift_out = _inproj(
        x2, shift0, p["norm_pre"], p["mu_shift"], cos, s1, s2, p["w_in"],
        batch=batch, seq=seq, tm=tm)

    if k_out.shape[0] == batch:
        k_out = k_out.reshape(batch, heads, 2, HEAD_QK, seq).transpose(0, 4, 1, 2, 3)
    else:
        k_out = k_out.reshape(batch, seq, heads, 2, HEAD_QK)
    k3 = kb2.reshape(batch, seq, width)
    v3 = v2.reshape(batch, seq, width)
    if past_k is None:
        k_all, v_all, q_off = k3, v3, 0
        tq, tk = min(1024, seq), min(256, seq)
    else:
        past = past_k.shape[1]
        total = past + seq
        padded = -(-total // LANES) * LANES
        zpad = jnp.zeros((batch, padded - total, width), F32)
        k_all = jnp.concatenate([past_k.reshape(batch, past, width).astype(BF16), k3,
                                 zpad.astype(BF16)], axis=1)
        v_all = jnp.concatenate([past_v.reshape(batch, past, width), v3, zpad], axis=1)
        q_off, tq, tk = past, seq, padded
    ya2 = _attention(p["lam4"], p["subln"], q2, k_all, v_all, ga2, batch=batch, tq_len=seq,
                     tq=tq, tk=tk, q_off=q_off, lambda_init=lambda_init)

    yb2, s_groups = _rwkv(m2, gb2, _group_states(wkv0), p["w_comb"], p["w0"], p["a0"], p["k_k"],
                          p["k_a"], p["r_k"], p["ln_x_w"], p["ln_x_b"], batch=batch, seq=seq)

    y2 = _outproj(x2, ya2, yb2, p["w_out_a"], p["w_out_b"], p["norm_post"], tm=tm)
    return (y2.reshape(batch, seq, d),
            k_out,
            v3.reshape(batch, seq, heads, HEAD_V),
            _ungroup_states(s_groups),
            shift_out)


def kernel(x_prompt, x_sample, cache_k, cache_v, state_wkv, state_shift, norm_pre, w_in, lam_q1, lam_k1, lam_q2, lam_k2, subln, mu_shift, w0, w_up, a0, a_up, k_k, k_a, r_k, ln_x_w, ln_x_b, w_out, norm_post):
    depth = w_in.shape[0]
    assert depth == 1, "single-layer problem"
    l = 0
    lambda_init = 0.8 - 0.6 * math.exp(-0.3 * l)
    width = w_up.shape[2]
    row = lambda a: a[l].reshape(1, -1).astype(F32)
    zeros = jnp.zeros((LORA, width), F32)
    p = {
        "norm_pre": row(norm_pre), "norm_post": row(norm_post), "subln": row(subln),
        "mu_shift": row(mu_shift), "w0": row(w0), "a0": row(a0), "k_k": row(k_k), "k_a": row(k_a),
        "r_k": row(r_k), "ln_x_w": row(ln_x_w), "ln_x_b": row(ln_x_b),
        "w_in": w_in[l].astype(BF16),
        "w_out_a": w_out[l, :width].astype(BF16), "w_out_b": w_out[l, width:].astype(BF16),
        "w_comb": jnp.concatenate([jnp.concatenate([w_up[l], zeros], axis=1),
                                   jnp.concatenate([zeros, a_up[l]], axis=1)], axis=0).astype(BF16),
        "lam4": jnp.stack([lam_q1[l], lam_k1[l], lam_q2[l], lam_k2[l]]).astype(F32),
    }
    b_p, t_p, _ = x_prompt.shape
    b_s, t_s, _ = x_sample.shape
    past_len = cache_k.shape[2]
    shift_w = state_shift.shape[-1]
    n_heads_b = state_wkv.shape[2]
    pos_p = jnp.arange(t_p, dtype=jnp.int32)
    pos_s = past_len + jnp.arange(t_s, dtype=jnp.int32)

    yp, kp, vp, wp, sp = _stream(
        x_prompt, pos_p, None, None,
        jnp.zeros((b_p, n_heads_b, RW_HEAD, RW_HEAD), F32), jnp.zeros((b_p, 1, shift_w), F32),
        p, lambda_init)
    ys, ks, vs, ws, ss = _stream(
        x_sample, pos_s, cache_k[l], cache_v[l], state_wkv[l].astype(F32), state_shift[l],
        p, lambda_init)
    return (yp, ys, kp[None], vp[None], wp[None], sp[None],
            ks[None], vs[None], ws[None].astype(state_wkv.dtype), ss[None])
```

```python
import functools
import itertools
import math

import jax
import jax.numpy as jnp
from jax import lax
from jax.experimental import pallas as pl
from jax.experimental.pallas import tpu as pltpu

F32 = jnp.float32
BF16 = jnp.bfloat16

CHUNK = 64
HEAD_V = 128
HEAD_QK = 64
ROT_DIM = 16
ROPE_THETA = 500000.0
RW_HEAD = 64
RW_CHUNK = 64
LORA = 64
NORM_EPS = 1e-6
SUBLN_EPS = 1e-5
GN_EPS = 64e-5
LANES = 128
MXU_N = 256
VMEM_LIMIT = 56 * 1024 * 1024
Q_SCALE = HEAD_QK ** -0.5 * math.log2(math.e)


def _sigmoid(x):
    return 1.0 / (1.0 + jnp.exp(-x))


def _softplus(x):
    return jnp.maximum(x, 0.0) + jnp.log1p(jnp.exp(-jnp.abs(x)))


def _dot(a, b):
    return jnp.dot(a, b, preferred_element_type=F32)


def _dot_nt(a, b):
    return lax.dot_general(a, b, (((1,), (1,)), ((), ())), preferred_element_type=F32)


def _dot_tn(a, b):
    return lax.dot_general(a, b, (((0,), (0,)), ((), ())), preferred_element_type=F32)


def _inproj_body(x_ref, shift0_ref, npre_ref, mu_ref, cos_ref, s1_ref, s2_ref, w_ref,
                 q_ref, k_ref, kb_ref, v_ref, vt_ref, ga_ref, m_ref, gb_ref, shift_ref,
                 h_scr, carry_scr, *, tiles_per_batch, tm, width, shift_w, col_chunk, transposed):
    i = pl.program_id(0)

    @pl.when(i == 0)
    def _():
        carry_scr[...] = jnp.zeros_like(carry_scr)

    x = x_ref[...]
    y = x * lax.rsqrt(jnp.mean(x * x, axis=-1, keepdims=True) + NORM_EPS)
    h_scr[...] = (y * npre_ref[...]).astype(BF16)

    def proj(c0, cw):
        return _dot(h_scr[...], w_ref[:, c0:c0 + cw])

    cos = cos_ref[...]
    s1 = s1_ref[...]
    s2 = s2_ref[...]

    def rope(p):
        outs = []
        for g in range(p.shape[1] // LANES):
            xg = p[:, g * LANES:(g + 1) * LANES]
            outs.append(xg * cos + pltpu.roll(xg, ROT_DIM // 2, axis=1) * s1
                        + pltpu.roll(xg, LANES - ROT_DIM // 2, axis=1) * s2)
        return jnp.concatenate(outs, axis=1)

    n_sub = width // col_chunk
    a_q, a_k, a_v, a_g = 0, width, 2 * width, 3 * width
    b_s = 4 * width
    b_g = b_s + shift_w
    for c in range(n_sub):
        o = c * col_chunk
        sl = slice(o, o + col_chunk)
        q = rope(proj(a_q + o, col_chunk)) * Q_SCALE
        k = rope(proj(a_k + o, col_chunk))
        v = proj(a_v + o, col_chunk)
        kb_ref[:, sl] = k.astype(BF16)
        v_ref[:, sl] = v
        if transposed:
            q_ref[0, sl, :] = q.T.astype(BF16)
            k_ref[0, sl, :] = k.T
            vt_ref[0, sl, :] = v.T.astype(BF16)
        else:
            q_ref[:, sl] = q.astype(BF16)
            k_ref[:, sl] = k
        g = proj(a_g + o, col_chunk)
        ga_ref[:, sl] = (g * _sigmoid(g)).astype(BF16)
        g = proj(b_g + o, col_chunk)
        gb_ref[:, sl] = (g * _sigmoid(g)).astype(BF16)
    if not transposed:
        vt_ref[...] = jnp.zeros_like(vt_ref)

    is_first = (i % tiles_per_batch) == 0
    o = 0
    while o < shift_w:
        cw = min(col_chunk, shift_w - o)
        sl = slice(o, o + cw)
        ps = proj(b_s + o, cw)
        prev_row = jnp.where(is_first, shift0_ref[0, :, sl], carry_scr[:, sl])
        rows = lax.broadcasted_iota(jnp.int32, ps.shape, 0)
        prev = jnp.where(rows == 0, prev_row, pltpu.roll(ps, 1, axis=0))
        last = ps[tm - 1:tm, :]
        carry_scr[:, sl] = last
        shift_ref[0, :, sl] = last
        m_ref[:, sl] = (ps + (prev - ps) * mu_ref[:, sl]).astype(BF16)
        o += cw


def _inproj(x2, shift0, npre, mu, cos, s1, s2, w_bf, *, batch, seq, tm):
    rows, d = x2.shape
    width = (w_bf.shape[1] - 2 * LORA) // 8
    shift_w = 3 * width + 2 * LORA
    tpb = seq // tm
    transposed = tm % LANES == 0
    body = functools.partial(_inproj_body, tiles_per_batch=tpb, tm=tm, width=width,
                             shift_w=shift_w, col_chunk=512, transposed=transposed)
    if transposed:
        t_spec = pl.BlockSpec((1, width, tm), lambda i: (i // tpb, 0, i % tpb))
        t_shape = lambda dt: jax.ShapeDtypeStruct((batch, width, seq), dt)
    else:
        t_spec = pl.BlockSpec((tm, width), lambda i: (i, 0))
        t_shape = lambda dt: jax.ShapeDtypeStruct((rows, width), dt)
    vt_spec = t_spec if transposed else pl.BlockSpec((8, LANES), lambda i: (0, 0))
    vt_shape = t_shape(BF16) if transposed else jax.ShapeDtypeStruct((8, LANES), BF16)
    row_blk = lambda w: pl.BlockSpec((tm, w), lambda i: (i, 0))
    full = lambda a: pl.BlockSpec(a.shape, lambda i: (0,) * a.ndim)
    tab = pl.BlockSpec((tm, LANES), lambda i: (i % tpb, 0))
    per_b = pl.BlockSpec((1, 1, shift_w), lambda i: (i // tpb, 0, 0))
    return pl.pallas_call(
        body,
        grid=(rows // tm,),
        in_specs=[row_blk(d), per_b, full(npre), full(mu), tab, tab, tab,
                  pl.BlockSpec(memory_space=pltpu.VMEM)],
        out_specs=[t_spec, t_spec, row_blk(width), row_blk(width), vt_spec, row_blk(width),
                   row_blk(shift_w), row_blk(width), per_b],
        out_shape=[t_shape(BF16),
                   t_shape(F32),
                   jax.ShapeDtypeStruct((rows, width), BF16),
                   jax.ShapeDtypeStruct((rows, width), F32),
                   vt_shape,
                   jax.ShapeDtypeStruct((rows, width), BF16),
                   jax.ShapeDtypeStruct((rows, shift_w), BF16),
                   jax.ShapeDtypeStruct((rows, width), BF16),
                   jax.ShapeDtypeStruct((batch, 1, shift_w), F32)],
        scratch_shapes=[pltpu.VMEM((tm, d), BF16), pltpu.VMEM((1, shift_w), F32)],
        compiler_params=pltpu.CompilerParams(dimension_semantics=("arbitrary",),
                                             vmem_limit_bytes=VMEM_LIMIT),
        name="inproj",
    )(x2, shift0, npre, mu, cos, s1, s2, w_bf)


def _attn_body(lam_ref, subln_ref, q_ref, k_ref, v_ref, ga_ref, ya_ref,
               vt_scr, qt_scr, sa_scr, sb_scr, cma_scr, cmb_scr, m_scr, acc_scr,
               *, tq, tk, qb, nq, q_off, lambda_init, transposed):
    qi = pl.program_id(2)
    n_kv = vt_scr.shape[0]
    n_blk = 2 * tq // qb
    n_diag = max(tq // tk, 1)
    v_rows = vt_scr.shape[1]

    @pl.when(qi == 0)
    def _():
        ones = jnp.ones((v_rows - HEAD_V, tk), BF16)
        if transposed:
            for c in range(n_kv):
                vt_scr[c, 0:HEAD_V, :] = v_ref[0, :, c * tk:(c + 1) * tk]
                vt_scr[c, HEAD_V:, :] = ones
        else:
            def cvt(c, carry):
                sl = pl.ds(pl.multiple_of(c * tk, tk), tk)
                vt_scr[c, 0:HEAD_V, :] = v_ref[0, sl, :].T.astype(BF16)
                vt_scr[c, HEAD_V:, :] = ones
                return carry
            lax.fori_loop(0, n_kv, cvt, 0)

    if transposed:
        qt = q_ref[0]
        feat = lax.broadcasted_iota(jnp.int32, qt.shape, 0)
        zero = jnp.zeros_like(qt)
        qst = jnp.concatenate([jnp.where(feat < HEAD_QK, qt, zero),
                               jnp.where(feat >= HEAD_QK, qt, zero)], axis=1)
        for c in range(n_blk):
            qt_scr[c] = qst[:, c * qb:(c + 1) * qb]
    else:
        q = q_ref[...].astype(F32)
        lane = lax.broadcasted_iota(jnp.int32, q.shape, 1)
        qs = jnp.concatenate([jnp.where(lane < HEAD_QK, q, 0.0),
                              jnp.where(lane >= HEAD_QK, q, 0.0)], axis=0)
        for c in range(n_blk):
            qt_scr[c] = qs[c * qb:(c + 1) * qb].T.astype(BF16)

    m_scr[...] = jnp.full_like(m_scr, -jnp.inf)
    acc_scr[...] = jnp.zeros_like(acc_scr)

    def diag_kind(t, c):
        q_lo = (c * qb) % tq
        if qb <= tq and q_lo + qb <= t * tk:
            return "skip"
        if qb <= tq and q_off == 0 and q_lo >= (t + 1) * tk:
            return None
        qrel = (c * qb + lax.broadcasted_iota(jnp.int32, (tk, qb), 1)) % tq
        krel = t * tk + lax.broadcasted_iota(jnp.int32, (tk, qb), 0)
        if q_off:
            return krel // CHUNK <= (q_off + qrel) // CHUNK
        return krel // CHUNK <= qrel // CHUNK

    bufs = ((sa_scr, cma_scr), (sb_scr, cmb_scr))

    def scores(j, buf, kinds):
        s_ref, cm_ref = bufs[buf]
        k_j = k_ref[0, pl.ds(pl.multiple_of(j * tk, tk), tk), :]
        for c in range(n_blk):
            if isinstance(kinds[c], str):
                continue
            s = _dot(k_j, qt_scr[c])
            s_ref[c] = s
            if kinds[c] is None:
                cm_ref[c] = jnp.max(s, axis=0, keepdims=True)

    def consume(j, buf, kinds):
        s_ref, cm_ref = bufs[buf]
        vt_j = vt_scr[j]
        for c in range(n_blk):
            if isinstance(kinds[c], str):
                continue
            s = s_ref[c]
            if kinds[c] is None:
                cm = cm_ref[c]
            else:
                s = jnp.where(kinds[c], s, -jnp.inf)
                cm = jnp.max(s, axis=0, keepdims=True)
            m_prev = m_scr[c]
            m_new = jnp.maximum(m_prev, cm)
            alpha = jnp.exp2(m_prev - m_new)
            p = jnp.exp2(s - m_new).astype(BF16)
            acc_scr[c] = alpha * acc_scr[c] + _dot(vt_j, p)
            m_scr[c] = m_new

    all_visible = [None] * n_blk
    diag = [[diag_kind(t, c) for c in range(n_blk)] for t in range(n_diag)]
    assert not any(isinstance(k, str) for k in diag[0])
    n_full = (qi * tq) // tk
    scores(0, 0, diag[0] if nq == 1 else all_visible)

    def tile_pair(j):
        scores(j + 1, 1, all_visible)
        consume(j, 0, all_visible)
        scores(j + 2, 0, all_visible)
        consume(j + 1, 1, all_visible)

    def quad_step(i, carry):
        tile_pair(4 * i)
        tile_pair(4 * i + 2)
        return carry

    n_pairs = n_full // 2
    if nq > 1:
        lax.fori_loop(0, n_pairs // 2, quad_step, 0)
        if (tq // tk) % 4:
            @pl.when(n_pairs % 2 == 1)
            def _():
                tile_pair(n_full - 2)

    for t in range(n_diag):
        if t + 1 < n_diag:
            scores(n_full + t + 1, (t + 1) % 2, diag[t + 1])
        consume(n_full + t, t % 2, diag[t])

    lam4 = lam_ref[...]
    e1 = jnp.exp(jnp.sum(lam4[0:1] * lam4[1:2], axis=1, keepdims=True))
    e2 = jnp.exp(jnp.sum(lam4[2:3] * lam4[3:4], axis=1, keepdims=True))
    lam = e1 - e2 + lambda_init
    acc = jnp.concatenate([acc_scr[c] for c in range(n_blk)], axis=1)
    o_all = acc[0:HEAD_V] * (1.0 / acc[HEAD_V:HEAD_V + 1])
    o = (o_all[:, :tq] - lam * o_all[:, tq:]).T
    y = o * lax.rsqrt(jnp.mean(o * o, axis=-1, keepdims=True) + SUBLN_EPS)
    y = (y * subln_ref[...]) * (1.0 - lambda_init)
    ya_ref[...] = (y * ga_ref[...].astype(F32)).astype(BF16)


def _attention(lam4, subln, q2, k3, v3, ga2, *, batch, tq_len, tq, tk, q_off, lambda_init):
    transposed = q2.ndim == 3
    tk_len = k3.shape[1]
    heads = k3.shape[2] // HEAD_V
    nq = tq_len // tq
    n_kv = tk_len // tk
    qb = min(MXU_N, 2 * tq)
    n_blk = 2 * tq // qb
    v_rows = HEAD_V + 16
    if q_off:
        assert nq == 1 and tk == tk_len and q_off % CHUNK == 0
    else:
        assert tq % tk == 0 and tq_len == tk_len and tq % CHUNK == 0
        assert nq == 1 or (tq // tk) % 2 == 0
    body = functools.partial(_attn_body, tq=tq, tk=tk, qb=qb, nq=nq, q_off=q_off,
                             lambda_init=lambda_init, transposed=transposed)
    qblk = pl.BlockSpec((tq, HEAD_V), lambda b, h, i: (b * nq + i, h))
    kvblk = pl.BlockSpec((1, tk_len, HEAD_V), lambda b, h, i: (b, 0, h))
    if transposed:
        q_in = pl.BlockSpec((1, HEAD_V, tq), lambda b, h, i: (b, h, i))
        v_in = pl.BlockSpec((1, HEAD_V, tk_len), lambda b, h, i: (b, h, 0))
    else:
        q_in, v_in = qblk, kvblk
    full = lambda a: pl.BlockSpec(a.shape, lambda b, h, i: (0,) * a.ndim)
    return pl.pallas_call(
        body,
        grid=(batch, heads, nq),
        in_specs=[full(lam4), full(subln), q_in, kvblk, v_in, qblk],
        out_specs=qblk,
        out_shape=jax.ShapeDtypeStruct(ga2.shape, BF16),
        scratch_shapes=[pltpu.VMEM((n_kv, v_rows, tk), BF16),
                        pltpu.VMEM((n_blk, HEAD_V, qb), BF16),
                        pltpu.VMEM((n_blk, tk, qb), F32),
                        pltpu.VMEM((n_blk, tk, qb), F32),
                        pltpu.VMEM((n_blk, 1, qb), F32),
                        pltpu.VMEM((n_blk, 1, qb), F32),
                        pltpu.VMEM((n_blk, 1, qb), F32),
                        pltpu.VMEM((n_blk, v_rows, qb), F32)],
        compiler_params=pltpu.CompilerParams(
            dimension_semantics=("arbitrary", "arbitrary", "arbitrary"),
            vmem_limit_bytes=VMEM_LIMIT),
        name="diffattn",
    )(lam4, subln, q2, k3, v3, ga2)


def _rwkv_body(m_ref, gb_ref, s0_ref, wcomb_ref, w0_ref, a0_ref, kk_ref, ka_ref, rk_ref,
               lnw_ref, lnb_ref, yb_ref, sout_ref, s_scr, *, width, n_sub):
    i = pl.program_id(1)
    L = RW_CHUNK
    gw = MXU_N
    hpg = gw // RW_HEAD
    n_grp = width // gw
    rng = range(n_grp)
    grp = [slice(g * gw, (g + 1) * gw) for g in rng]

    @pl.when(i == 0)
    def _():
        s_scr[...] = s0_ref[0]

    gi = lax.broadcasted_iota(jnp.int32, (gw, gw), 0) // RW_HEAD
    gj = lax.broadcasted_iota(jnp.int32, (gw, gw), 1) // RW_HEAD
    same_head = gi == gj
    group_ones = same_head.astype(BF16)

    def head_sums(*ts):
        rows = jnp.concatenate([t[:, s].astype(BF16) for t in ts for s in grp], axis=0)
        out = _dot(rows, group_ones)
        return [jnp.concatenate([out[(k * n_grp + g) * L:(k * n_grp + g + 1) * L] for g in rng],
                                axis=1) for k in range(len(ts))]

    lane_head = lax.broadcasted_iota(jnp.int32, (L, gw), 1) // RW_HEAD

    def stack(t):
        tb = t.astype(BF16)
        zero = jnp.zeros_like(tb)
        return jnp.concatenate([jnp.where(lane_head == h, tb, zero) for h in range(hpg)], axis=0)

    def prepare(sub, out):
        rows = slice(sub * L, (sub + 1) * L)
        r = m_ref[rows, 0:width].astype(F32)
        kb = m_ref[rows, width:2 * width].astype(F32)
        vb = m_ref[rows, 2 * width:3 * width]
        z = m_ref[rows, 3 * width:3 * width + 2 * LORA].astype(F32)
        lane = lax.broadcasted_iota(jnp.int32, (L, 2 * LORA), 1)
        zt = jnp.where(lane < LORA, jnp.tanh(z), z).astype(BF16)
        lin = _dot(zt, wcomb_ref[...])
        yield
        w_log = -_softplus(-(w0_ref[...] + lin[:, :width])) - 0.5
        logw = -jnp.exp(w_log)
        alpha = _sigmoid(a0_ref[...] + lin[:, width:])

        kk = kb * kk_ref[...]
        kb = kb * (1.0 + (alpha - 1.0) * ka_ref[...])
        kk_sq, bonus_dot = head_sums(kk * kk, r * kb * rk_ref[...])
        yield
        kk = kk / jnp.maximum(jnp.sqrt(kk_sq), 1e-12)

        ti = lax.broadcasted_iota(jnp.int32, (L, L), 0)
        tj = lax.broadcasted_iota(jnp.int32, (L, L), 1)
        tri = (ti >= tj).astype(BF16)
        lw_hi = logw.astype(BF16)
        lw_lo = (logw - lw_hi.astype(F32)).astype(BF16)
        lp = _dot(tri, lw_hi) + _dot(tri, lw_lo)
        yield
        p_inv = jnp.exp(-lp)
        r_hat = r * jnp.exp(lp)
        a_hat = -kk * jnp.exp(lp - logw)
        b_til = kk * alpha * p_inv
        k_til = kb * p_inv
        out.update(
            ar=[jnp.concatenate([a_hat[:, s], r_hat[:, s]], axis=0).astype(BF16) for s in grp],
            bk=[jnp.concatenate([stack(b_til[:, s]), stack(k_til[:, s])], axis=0) for s in grp],
            vs=[stack(vb[:, s]) for s in grp],
            bkr=[jnp.concatenate([b_til[:, s], k_til[:, s]], axis=0).astype(BF16) for s in grp],
            vb=vb, p_last=jnp.exp(lp[L - 1:L, :]), bonus=bonus_dot * vb.astype(F32))

    row = lax.broadcasted_iota(jnp.int32, (L, gw), 0)
    col = lax.broadcasted_iota(jnp.int32, (L, gw), 1) % L
    strict = col < row
    incl = col <= row
    eye = (col == row).astype(F32)

    def solve(pre, s_old, out):
        ar, vs = pre["ar"], pre["vs"]
        gram = [_dot_nt(ar[g], pre["bk"][g]) for g in rng]
        yield
        a_ab = [jnp.where(strict, gm[:L, :gw], 0.0) for gm in gram]
        a_ak = [jnp.where(strict, gm[:L, gw:], 0.0).astype(BF16) for gm in gram]
        a_r = [jnp.concatenate([jnp.where(incl, gm[L:, :gw], 0.0),
                                jnp.where(incl, gm[L:, gw:], 0.0)], axis=1).astype(BF16)
               for gm in gram]

        t_inv = [eye + a for a in a_ab]
        pw = [_dot(a.astype(BF16), stack(a)) for a in a_ab]
        yield
        for _ in range(int(math.log2(L)) - 2):
            res = [_dot(jnp.concatenate([pw[g], t_inv[g]], axis=0).astype(BF16), stack(pw[g]))
                   for g in rng]
            yield
            t_inv = [t_inv[g] + res[g][L:] for g in rng]
            pw = [res[g][:L] for g in rng]
        t_inv = [(t_inv[g] + _dot(t_inv[g].astype(BF16), stack(pw[g]))).astype(BF16) for g in rng]
        yield

        asr = [_dot_nt(ar[g], s_old[g].astype(BF16)) for g in rng]
        yield
        x = [asr[g][:L] + _dot(a_ak[g], vs[g]) for g in rng]
        yield
        u = [_dot(t_inv[g], stack(x[g])) for g in rng]
        yield
        ys = [asr[g][L:] + _dot(a_r[g], jnp.concatenate([stack(u[g]), vs[g]], axis=0))
              for g in rng]
        yield
        s_new = []
        for g, s in enumerate(grp):
            upd = _dot_tn(jnp.concatenate([u[g].astype(BF16), pre["vb"][:, s]], axis=0),
                          pre["bkr"][g])
            s_new.append((s_old[g] + jnp.where(same_head, upd, 0.0)) * pre["p_last"][:, s])
        out.update(s_new=s_new, yb=jnp.concatenate(ys, axis=1))

    def finish(sub, pre, yb):
        inv_n = 1.0 / RW_HEAD
        mean = head_sums(yb)[0] * inv_n
        yield
        d = yb - mean
        var = head_sums(d * d)[0] * inv_n
        yield
        yn = d * lax.rsqrt(var + GN_EPS) * lnw_ref[...] + lnb_ref[...]
        rows = slice(sub * L, (sub + 1) * L)
        yb_ref[rows, :] = ((yn + pre["bonus"]) * gb_ref[rows, :].astype(F32)).astype(BF16)

    def run(main, side=None):
        for _ in main:
            if side is not None and next(side, "done") == "done":
                side = None
        for _ in side or ():
            pass

    state = [s_scr[g] for g in rng]
    pre, res = [dict() for _ in range(n_sub)], [dict() for _ in range(n_sub)]
    run(prepare(0, pre[0]))
    for sub in range(n_sub):
        side = prepare(sub + 1, pre[sub + 1]) if sub + 1 < n_sub else None
        if sub > 0:
            side = itertools.chain(finish(sub - 1, pre[sub - 1], res[sub - 1]["yb"]), side or ())
        run(solve(pre[sub], state, res[sub]), side)
        state = res[sub]["s_new"]
    run(finish(n_sub - 1, pre[n_sub - 1], res[n_sub - 1]["yb"]))
    for g in rng:
        s_scr[g] = state[g]

    @pl.when(i == pl.num_programs(1) - 1)
    def _():
        sout_ref[0] = s_scr[...]


def _rwkv(m2, gb2, s0g, wcomb, w0, a0, k_k, k_a, r_k, ln_w, ln_b, *, batch, seq):
    rows, shift_w = m2.shape
    width = gb2.shape[1]
    nc = seq // RW_CHUNK
    n_sub = next(n for n in (8, 4, 2, 1) if nc % n == 0)
    n_steps = nc // n_sub
    n_grp = width // MXU_N
    body = functools.partial(_rwkv_body, width=width, n_sub=n_sub)
    row_blk = lambda w: pl.BlockSpec((n_sub * RW_CHUNK, w), lambda b, i: (b * n_steps + i, 0))
    full = lambda a: pl.BlockSpec(a.shape, lambda b, i: (0,) * a.ndim)
    st = pl.BlockSpec((1, n_grp, MXU_N, MXU_N), lambda b, i: (b, 0, 0, 0))
    return pl.pallas_call(
        body,
        grid=(batch, n_steps),
        in_specs=[row_blk(shift_w), row_blk(width), st, full(wcomb), full(w0), full(a0),
                  full(k_k), full(k_a), full(r_k), full(ln_w), full(ln_b)],
        out_specs=[row_blk(width), st],
        out_shape=[jax.ShapeDtypeStruct((rows, width), BF16),
                   jax.ShapeDtypeStruct(s0g.shape, F32)],
        scratch_shapes=[pltpu.VMEM((n_grp, MXU_N, MXU_N), F32)],
        compiler_params=pltpu.CompilerParams(dimension_semantics=("arbitrary", "arbitrary"),
                                             vmem_limit_bytes=VMEM_LIMIT),
        name="rwkv7",
    )(m2, gb2, s0g, wcomb, w0, a0, k_k, k_a, r_k, ln_w, ln_b)


def _outproj_body(x_ref, ya_ref, yb_ref, wa_ref, wb_ref, npost_ref, y_ref):
    out = _dot(ya_ref[...], wa_ref[...]) + _dot(yb_ref[...], wb_ref[...])
    y = out * lax.rsqrt(jnp.mean(out * out, axis=-1, keepdims=True) + NORM_EPS)
    y_ref[...] = x_ref[...] + y * npost_ref[...]


def _outproj(x2, ya2, yb2, wa, wb, npost, *, tm):
    rows, d = x2.shape
    row_blk = lambda w: pl.BlockSpec((tm, w), lambda i: (i, 0))
    full = lambda a: pl.BlockSpec(a.shape, lambda i: (0,) * a.ndim)
    return pl.pallas_call(
        _outproj_body,
        grid=(rows // tm,),
        in_specs=[row_blk(d), row_blk(ya2.shape[1]), row_blk(yb2.shape[1]),
                  full(wa), full(wb), full(npost)],
        out_specs=row_blk(d),
        out_shape=jax.ShapeDtypeStruct((rows, d), F32),
        compiler_params=pltpu.CompilerParams(dimension_semantics=("arbitrary",),
                                             vmem_limit_bytes=VMEM_LIMIT),
        name="outproj",
    )(x2, ya2, yb2, wa, wb, npost)


def _rope_tables(pos):
    half = ROT_DIM // 2
    inv = jnp.power(jnp.float32(ROPE_THETA), -jnp.arange(half, dtype=F32) * (2.0 / ROT_DIM))
    ang = pos.astype(F32)[:, None] * inv[None, :]
    cos, sin = jnp.cos(ang), jnp.sin(ang)
    n = pos.shape[0]
    pad = HEAD_QK - ROT_DIM
    zeros, ones = jnp.zeros((n, half), F32), jnp.ones((n, pad), F32)
    zpad = jnp.zeros((n, pad), F32)
    c = jnp.concatenate([cos, cos, ones], axis=1)
    s_prev = jnp.concatenate([zeros, sin, zpad], axis=1)
    s_next = jnp.concatenate([-sin, zeros, zpad], axis=1)
    rep = LANES // HEAD_QK
    return tuple(jnp.tile(t, (1, rep)) for t in (c, s_prev, s_next))


def _group_states(s):
    b, h, n, _ = s.shape
    hpg = MXU_N // n
    s = s.reshape(b, h // hpg, hpg, n, n)
    out = jnp.einsum("bghij,hk->bghikj", s, jnp.eye(hpg, dtype=s.dtype))
    return out.reshape(b, h // hpg, MXU_N, MXU_N)


def _ungroup_states(sg):
    n = RW_HEAD
    hpg = MXU_N // n
    b, g = sg.shape[:2]
    s = jnp.einsum("bghikj,hk->bghij", sg.reshape(b, g, hpg, n, hpg, n), jnp.eye(hpg, dtype=sg.dtype))
    return s.reshape(b, g * hpg, n, n)


def _stream(x, pos, past_k, past_v, wkv0, shift0, p, lambda_init):
    batch, seq, d = x.shape
    width = p["w_out_a"].shape[0]
    heads = width // HEAD_V
    rows = batch * seq
    x2 = x.reshape(rows, d)
    tm = min(512, seq)
    cos, s1, s2 = _rope_tables(pos)
    q2, k_out, kb2, v2, vt, ga2, m2, gb2, shift_out = _inproj(
        x2, shift0, p["norm_pre"], p["mu_shift"], cos, s1, s2, p["w_in"],
        batch=batch, seq=seq, tm=tm)

    if k_out.shape[0] == batch:
        k_out = k_out.reshape(batch, heads, 2, HEAD_QK, seq).transpose(0, 4, 1, 2, 3)
    else:
        k_out = k_out.reshape(batch, seq, heads, 2, HEAD_QK)
    k3 = kb2.reshape(batch, seq, width)
    v3 = v2.reshape(batch, seq, width)
    if past_k is None:
        k_all, v_all, q_off = k3, (vt if q2.ndim == 3 else v3), 0
        tq, tk = min(1024, seq), min(256, seq)
    else:
        past = past_k.shape[1]
        total = past + seq
        padded = -(-total // LANES) * LANES
        zpad = jnp.zeros((batch, padded - total, width), F32)
        k_all = jnp.concatenate([past_k.reshape(batch, past, width).astype(BF16), k3,
                                 zpad.astype(BF16)], axis=1)
        v_all = jnp.concatenate([past_v.reshape(batch, past, width), v3, zpad], axis=1)
        q_off, tq, tk = past, seq, padded
    ya2 = _attention(p["lam4"], p["subln"], q2, k_all, v_all, ga2, batch=batch, tq_len=seq,
                     tq=tq, tk=tk, q_off=q_off, lambda_init=lambda_init)

    yb2, s_groups = _rwkv(m2, gb2, _group_states(wkv0), p["w_comb"], p["w0"], p["a0"], p["k_k"],
                          p["k_a"], p["r_k"], p["ln_x_w"], p["ln_x_b"], batch=batch, seq=seq)

    y2 = _outproj(x2, ya2, yb2, p["w_out_a"], p["w_out_b"], p["norm_post"], tm=tm)
    return (y2.reshape(batch, seq, d),
            k_out,
            v3.reshape(batch, seq, heads, HEAD_V),
            _ungroup_states(s_groups),
            shift_out)


def kernel(x_prompt, x_sample, cache_k, cache_v, state_wkv, state_shift, norm_pre, w_in, lam_q1, lam_k1, lam_q2, lam_k2, subln, mu_shift, w0, w_up, a0, a_up, k_k, k_a, r_k, ln_x_w, ln_x_b, w_out, norm_post):
    depth = w_in.shape[0]
    assert depth == 1, "single-layer problem"
    l = 0
    lambda_init = 0.8 - 0.6 * math.exp(-0.3 * l)
    width = w_up.shape[2]
    row = lambda a: a[l].reshape(1, -1).astype(F32)
    zeros = jnp.zeros((LORA, width), F32)
    p = {
        "norm_pre": row(norm_pre), "norm_post": row(norm_post), "subln": row(subln),
        "mu_shift": row(mu_shift), "w0": row(w0), "a0": row(a0), "k_k": row(k_k), "k_a": row(k_a),
        "r_k": row(r_k), "ln_x_w": row(ln_x_w), "ln_x_b": row(ln_x_b),
        "w_in": w_in[l].astype(BF16),
        "w_out_a": w_out[l, :width].astype(BF16), "w_out_b": w_out[l, width:].astype(BF16),
        "w_comb": jnp.concatenate([jnp.concatenate([w_up[l], zeros], axis=1),
                                   jnp.concatenate([zeros, a_up[l]], axis=1)], axis=0).astype(BF16),
        "lam4": jnp.stack([lam_q1[l], lam_k1[l], lam_q2[l], lam_k2[l]]).astype(F32),
    }
    b_p, t_p, _ = x_prompt.shape
    b_s, t_s, _ = x_sample.shape
    past_len = cache_k.shape[2]
    shift_w = state_shift.shape[-1]
    n_heads_b = state_wkv.shape[2]
    pos_p = jnp.arange(t_p, dtype=jnp.int32)
    pos_s = past_len + jnp.arange(t_s, dtype=jnp.int32)

    yp, kp, vp, wp, sp = _stream(
        x_prompt, pos_p, None, None,
        jnp.zeros((b_p, n_heads_b, RW_HEAD, RW_HEAD), F32), jnp.zeros((b_p, 1, shift_w), F32),
        p, lambda_init)
    ys, ks, vs, ws, ss = _stream(
        x_sample, pos_s, cache_k[l], cache_v[l], state_wkv[l].astype(F32), state_shift[l],
        p, lambda_init)
    return (yp, ys, kp[None], vp[None], wp[None], sp[None],
            ks[None], vs[None], ws[None].astype(state_wkv.dtype), ss[None])
```

```python
import functools
import itertools
import math

import jax
import jax.numpy as jnp
from jax import lax
from jax.experimental import pallas as pl
from jax.experimental.pallas import tpu as pltpu

F32 = jnp.float32
BF16 = jnp.bfloat16

CHUNK = 64
HEAD_V = 128
HEAD_QK = 64
ROT_DIM = 16
ROPE_THETA = 500000.0
RW_HEAD = 64
RW_CHUNK = 64
LORA = 64
NORM_EPS = 1e-6
SUBLN_EPS = 1e-5
GN_EPS = 64e-5
LANES = 128
MXU_N = 256
VMEM_LIMIT = 56 * 1024 * 1024
Q_SCALE = HEAD_QK ** -0.5 * math.log2(math.e)


def _sigmoid(x):
    return 1.0 / (1.0 + jnp.exp(-x))


def _softplus(x):
    return jnp.maximum(x, 0.0) + jnp.log1p(jnp.exp(-jnp.abs(x)))


def _dot(a, b):
    return jnp.dot(a, b, preferred_element_type=F32)


def _dot_nt(a, b):
    return lax.dot_general(a, b, (((1,), (1,)), ((), ())), preferred_element_type=F32)


def _dot_tn(a, b):
    return lax.dot_general(a, b, (((0,), (0,)), ((), ())), preferred_element_type=F32)


def _inproj_body(x_ref, shift0_ref, npre_ref, mu_ref, cos_ref, s1_ref, s2_ref, w_ref,
                 q_ref, k_ref, kb_ref, v_ref, vt_ref, ga_ref, m_ref, gb_ref, shift_ref,
                 h_scr, carry_scr, *, tiles_per_batch, tm, width, shift_w, col_chunk, transposed):
    i = pl.program_id(0)

    @pl.when(i == 0)
    def _():
        carry_scr[...] = jnp.zeros_like(carry_scr)

    x = x_ref[...]
    y = x * lax.rsqrt(jnp.mean(x * x, axis=-1, keepdims=True) + NORM_EPS)
    h_scr[...] = (y * npre_ref[...]).astype(BF16)

    def proj(c0, cw):
        return _dot(h_scr[...], w_ref[:, c0:c0 + cw])

    cos = cos_ref[...]
    s1 = s1_ref[...]
    s2 = s2_ref[...]

    def rope(p):
        outs = []
        for g in range(p.shape[1] // LANES):
            xg = p[:, g * LANES:(g + 1) * LANES]
            outs.append(xg * cos + pltpu.roll(xg, ROT_DIM // 2, axis=1) * s1
                        + pltpu.roll(xg, LANES - ROT_DIM // 2, axis=1) * s2)
        return jnp.concatenate(outs, axis=1)

    n_sub = width // col_chunk
    a_q, a_k, a_v, a_g = 0, width, 2 * width, 3 * width
    b_s = 4 * width
    b_g = b_s + shift_w
    for c in range(n_sub):
        o = c * col_chunk
        sl = slice(o, o + col_chunk)
        q = rope(proj(a_q + o, col_chunk)) * Q_SCALE
        k = rope(proj(a_k + o, col_chunk))
        v = proj(a_v + o, col_chunk)
        kb_ref[:, sl] = k.astype(BF16)
        v_ref[:, sl] = v
        if transposed:
            q_ref[0, sl, :] = q.T.astype(BF16)
            k_ref[0, sl, :] = k.T
            vt_ref[0, sl, :] = v.T.astype(BF16)
        else:
            q_ref[:, sl] = q.astype(BF16)
            k_ref[:, sl] = k
        g = proj(a_g + o, col_chunk)
        ga_ref[:, sl] = (g * _sigmoid(g)).astype(BF16)
        g = proj(b_g + o, col_chunk)
        gb_ref[:, sl] = (g * _sigmoid(g)).astype(BF16)
    if not transposed:
        vt_ref[...] = jnp.zeros_like(vt_ref)

    is_first = (i % tiles_per_batch) == 0
    o = 0
    while o < shift_w:
        cw = min(col_chunk, shift_w - o)
        sl = slice(o, o + cw)
        ps = proj(b_s + o, cw)
        prev_row = jnp.where(is_first, shift0_ref[0, :, sl], carry_scr[:, sl])
        rows = lax.broadcasted_iota(jnp.int32, ps.shape, 0)
        prev = jnp.where(rows == 0, prev_row, pltpu.roll(ps, 1, axis=0))
        last = ps[tm - 1:tm, :]
        carry_scr[:, sl] = last
        shift_ref[0, :, sl] = last
        m_ref[:, sl] = (ps + (prev - ps) * mu_ref[:, sl]).astype(BF16)
        o += cw


def _inproj(x2, shift0, npre, mu, cos, s1, s2, w_bf, *, batch, seq, tm):
    rows, d = x2.shape
    width = (w_bf.shape[1] - 2 * LORA) // 8
    shift_w = 3 * width + 2 * LORA
    tpb = seq // tm
    transposed = tm % LANES == 0
    body = functools.partial(_inproj_body, tiles_per_batch=tpb, tm=tm, width=width,
                             shift_w=shift_w, col_chunk=512, transposed=transposed)
    if transposed:
        t_spec = pl.BlockSpec((1, width, tm), lambda i: (i // tpb, 0, i % tpb))
        t_shape = lambda dt: jax.ShapeDtypeStruct((batch, width, seq), dt)
    else:
        t_spec = pl.BlockSpec((tm, width), lambda i: (i, 0))
        t_shape = lambda dt: jax.ShapeDtypeStruct((rows, width), dt)
    vt_spec = t_spec if transposed else pl.BlockSpec((8, LANES), lambda i: (0, 0))
    vt_shape = t_shape(BF16) if transposed else jax.ShapeDtypeStruct((8, LANES), BF16)
    row_blk = lambda w: pl.BlockSpec((tm, w), lambda i: (i, 0))
    full = lambda a: pl.BlockSpec(a.shape, lambda i: (0,) * a.ndim)
    tab = pl.BlockSpec((tm, LANES), lambda i: (i % tpb, 0))
    per_b = pl.BlockSpec((1, 1, shift_w), lambda i: (i // tpb, 0, 0))
    return pl.pallas_call(
        body,
        grid=(rows // tm,),
        in_specs=[row_blk(d), per_b, full(npre), full(mu), tab, tab, tab,
                  pl.BlockSpec(memory_space=pltpu.VMEM)],
        out_specs=[t_spec, t_spec, row_blk(width), row_blk(width), vt_spec, row_blk(width),
                   row_blk(shift_w), row_blk(width), per_b],
        out_shape=[t_shape(BF16),
                   t_shape(F32),
                   jax.ShapeDtypeStruct((rows, width), BF16),
                   jax.ShapeDtypeStruct((rows, width), F32),
                   vt_shape,
                   jax.ShapeDtypeStruct((rows, width), BF16),
                   jax.ShapeDtypeStruct((rows, shift_w), BF16),
                   jax.ShapeDtypeStruct((rows, width), BF16),
                   jax.ShapeDtypeStruct((batch, 1, shift_w), F32)],
        scratch_shapes=[pltpu.VMEM((tm, d), BF16), pltpu.VMEM((1, shift_w), F32)],
        compiler_params=pltpu.CompilerParams(dimension_semantics=("arbitrary",),
                                             vmem_limit_bytes=VMEM_LIMIT),
        name="inproj",
    )(x2, shift0, npre, mu, cos, s1, s2, w_bf)


def _attn_body(lam_ref, subln_ref, q_ref, k_ref, v_ref, ga_ref, ya_ref,
               vt_scr, qt_scr, sa_scr, sb_scr, cma_scr, cmb_scr, m_scr, acc_scr,
               *, tq, tk, qb, nq, q_off, lambda_init, transposed):
    qi = pl.program_id(2)
    n_kv = vt_scr.shape[0]
    n_blk = 2 * tq // qb
    n_diag = max(tq // tk, 1)
    v_rows = vt_scr.shape[1]

    @pl.when(qi == 0)
    def _():
        ones = jnp.ones((v_rows - HEAD_V, tk), BF16)
        if transposed:
            for c in range(n_kv):
                vt_scr[c, 0:HEAD_V, :] = v_ref[0, :, c * tk:(c + 1) * tk]
                vt_scr[c, HEAD_V:, :] = ones
        else:
            def cvt(c, carry):
                sl = pl.ds(pl.multiple_of(c * tk, tk), tk)
                vt_scr[c, 0:HEAD_V, :] = v_ref[0, sl, :].T.astype(BF16)
                vt_scr[c, HEAD_V:, :] = ones
                return carry
            lax.fori_loop(0, n_kv, cvt, 0)

    if transposed:
        qt = q_ref[0]
        feat = lax.broadcasted_iota(jnp.int32, qt.shape, 0)
        zero = jnp.zeros_like(qt)
        qst = jnp.concatenate([jnp.where(feat < HEAD_QK, qt, zero),
                               jnp.where(feat >= HEAD_QK, qt, zero)], axis=1)
        for c in range(n_blk):
            qt_scr[c] = qst[:, c * qb:(c + 1) * qb]
    else:
        q = q_ref[...].astype(F32)
        lane = lax.broadcasted_iota(jnp.int32, q.shape, 1)
        qs = jnp.concatenate([jnp.where(lane < HEAD_QK, q, 0.0),
                              jnp.where(lane >= HEAD_QK, q, 0.0)], axis=0)
        for c in range(n_blk):
            qt_scr[c] = qs[c * qb:(c + 1) * qb].T.astype(BF16)

    m_scr[...] = jnp.full_like(m_scr, -jnp.inf)
    acc_scr[...] = jnp.zeros_like(acc_scr)

    def diag_kind(t, c):
        q_lo = (c * qb) % tq
        if qb <= tq and q_lo + qb <= t * tk:
            return "skip"
        if qb <= tq and q_off == 0 and q_lo >= (t + 1) * tk:
            return None
        qrel = (c * qb + lax.broadcasted_iota(jnp.int32, (tk, qb), 1)) % tq
        krel = t * tk + lax.broadcasted_iota(jnp.int32, (tk, qb), 0)
        if q_off:
            return krel // CHUNK <= (q_off + qrel) // CHUNK
        return krel // CHUNK <= qrel // CHUNK

    bufs = ((sa_scr, cma_scr), (sb_scr, cmb_scr))

    def scores(j, buf, kinds):
        s_ref, cm_ref = bufs[buf]
        k_j = k_ref[0, pl.ds(pl.multiple_of(j * tk, tk), tk), :]
        for c in range(n_blk):
            if isinstance(kinds[c], str):
                continue
            s = _dot(k_j, qt_scr[c])
            s_ref[c] = s
            if kinds[c] is None:
                cm_ref[c] = jnp.max(s, axis=0, keepdims=True)

    def consume(j, buf, kinds):
        s_ref, cm_ref = bufs[buf]
        vt_j = vt_scr[j]
        for c in range(n_blk):
            if isinstance(kinds[c], str):
                continue
            s = s_ref[c]
            if kinds[c] is None:
                cm = cm_ref[c]
            else:
                s = jnp.where(kinds[c], s, -jnp.inf)
                cm = jnp.max(s, axis=0, keepdims=True)
            m_prev = m_scr[c]
            m_new = jnp.maximum(m_prev, cm)
            alpha = jnp.exp2(m_prev - m_new)
            p = jnp.exp2(s - m_new).astype(BF16)
            acc_scr[c] = alpha * acc_scr[c] + _dot(vt_j, p)
            m_scr[c] = m_new

    all_visible = [None] * n_blk
    diag = [[diag_kind(t, c) for c in range(n_blk)] for t in range(n_diag)]
    assert not any(isinstance(k, str) for k in diag[0])
    n_full = (qi * tq) // tk
    scores(0, 0, diag[0] if nq == 1 else all_visible)

    def tile_pair(j):
        scores(j + 1, 1, all_visible)
        consume(j, 0, all_visible)
        scores(j + 2, 0, all_visible)
        consume(j + 1, 1, all_visible)

    def quad(j):
        tile_pair(j)
        tile_pair(j + 2)

    def oct_step(i, carry):
        quad(8 * i)
        quad(8 * i + 4)
        return carry

    n_pairs = n_full // 2
    if nq > 1:
        n_quads = n_pairs // 2
        lax.fori_loop(0, n_quads // 2, oct_step, 0)

        @pl.when(n_quads % 2 == 1)
        def _():
            quad(4 * (n_quads - 1))

        if (tq // tk) % 4:
            @pl.when(n_pairs % 2 == 1)
            def _():
                tile_pair(n_full - 2)

    for t in range(n_diag):
        if t + 1 < n_diag:
            scores(n_full + t + 1, (t + 1) % 2, diag[t + 1])
        consume(n_full + t, t % 2, diag[t])

    lam4 = lam_ref[...]
    e1 = jnp.exp(jnp.sum(lam4[0:1] * lam4[1:2], axis=1, keepdims=True))
    e2 = jnp.exp(jnp.sum(lam4[2:3] * lam4[3:4], axis=1, keepdims=True))
    lam = e1 - e2 + lambda_init
    acc = jnp.concatenate([acc_scr[c] for c in range(n_blk)], axis=1)
    o_all = acc[0:HEAD_V] * (1.0 / acc[HEAD_V:HEAD_V + 1])
    o = (o_all[:, :tq] - lam * o_all[:, tq:]).T
    y = o * lax.rsqrt(jnp.mean(o * o, axis=-1, keepdims=True) + SUBLN_EPS)
    y = (y * subln_ref[...]) * (1.0 - lambda_init)
    ya_ref[...] = (y * ga_ref[...].astype(F32)).astype(BF16)


def _attention(lam4, subln, q2, k3, v3, ga2, *, batch, tq_len, tq, tk, q_off, lambda_init):
    transposed = q2.ndim == 3
    tk_len = k3.shape[1]
    heads = k3.shape[2] // HEAD_V
    nq = tq_len // tq
    n_kv = tk_len // tk
    qb = min(MXU_N, 2 * tq)
    n_blk = 2 * tq // qb
    v_rows = HEAD_V + 16
    if q_off:
        assert nq == 1 and tk == tk_len and q_off % CHUNK == 0
    else:
        assert tq % tk == 0 and tq_len == tk_len and tq % CHUNK == 0
        assert nq == 1 or (tq // tk) % 2 == 0
    body = functools.partial(_attn_body, tq=tq, tk=tk, qb=qb, nq=nq, q_off=q_off,
                             lambda_init=lambda_init, transposed=transposed)
    qblk = pl.BlockSpec((tq, HEAD_V), lambda b, h, i: (b * nq + i, h))
    kvblk = pl.BlockSpec((1, tk_len, HEAD_V), lambda b, h, i: (b, 0, h))
    if transposed:
        q_in = pl.BlockSpec((1, HEAD_V, tq), lambda b, h, i: (b, h, i))
        v_in = pl.BlockSpec((1, HEAD_V, tk_len), lambda b, h, i: (b, h, 0))
    else:
        q_in, v_in = qblk, kvblk
    full = lambda a: pl.BlockSpec(a.shape, lambda b, h, i: (0,) * a.ndim)
    return pl.pallas_call(
        body,
        grid=(batch, heads, nq),
        in_specs=[full(lam4), full(subln), q_in, kvblk, v_in, qblk],
        out_specs=qblk,
        out_shape=jax.ShapeDtypeStruct(ga2.shape, BF16),
        scratch_shapes=[pltpu.VMEM((n_kv, v_rows, tk), BF16),
                        pltpu.VMEM((n_blk, HEAD_V, qb), BF16),
                        pltpu.VMEM((n_blk, tk, qb), F32),
                        pltpu.VMEM((n_blk, tk, qb), F32),
                        pltpu.VMEM((n_blk, 1, qb), F32),
                        pltpu.VMEM((n_blk, 1, qb), F32),
                        pltpu.VMEM((n_blk, 1, qb), F32),
                        pltpu.VMEM((n_blk, v_rows, qb), F32)],
        compiler_params=pltpu.CompilerParams(
            dimension_semantics=("arbitrary", "arbitrary", "arbitrary"),
            vmem_limit_bytes=VMEM_LIMIT),
        name="diffattn",
    )(lam4, subln, q2, k3, v3, ga2)


def _rwkv_body(m_ref, gb_ref, s0_ref, wcomb_ref, w0_ref, a0_ref, kk_ref, ka_ref, rk_ref,
               lnw_ref, lnb_ref, yb_ref, sout_ref, s_scr, *, width, n_sub):
    i = pl.program_id(1)
    L = RW_CHUNK
    gw = MXU_N
    hpg = gw // RW_HEAD
    n_grp = width // gw
    rng = range(n_grp)
    grp = [slice(g * gw, (g + 1) * gw) for g in rng]

    @pl.when(i == 0)
    def _():
        s_scr[...] = s0_ref[0]

    gi = lax.broadcasted_iota(jnp.int32, (gw, gw), 0) // RW_HEAD
    gj = lax.broadcasted_iota(jnp.int32, (gw, gw), 1) // RW_HEAD
    same_head = gi == gj
    group_ones = same_head.astype(BF16)

    def head_sums(*ts):
        rows = jnp.concatenate([t[:, s].astype(BF16) for t in ts for s in grp], axis=0)
        out = _dot(rows, group_ones)
        return [jnp.concatenate([out[(k * n_grp + g) * L:(k * n_grp + g + 1) * L] for g in rng],
                                axis=1) for k in range(len(ts))]

    lane_head = lax.broadcasted_iota(jnp.int32, (L, gw), 1) // RW_HEAD

    def stack(t):
        tb = t.astype(BF16)
        zero = jnp.zeros_like(tb)
        return jnp.concatenate([jnp.where(lane_head == h, tb, zero) for h in range(hpg)], axis=0)

    def prepare(sub, out):
        rows = slice(sub * L, (sub + 1) * L)
        r = m_ref[rows, 0:width].astype(F32)
        kb = m_ref[rows, width:2 * width].astype(F32)
        vb = m_ref[rows, 2 * width:3 * width]
        z = m_ref[rows, 3 * width:3 * width + 2 * LORA].astype(F32)
        lane = lax.broadcasted_iota(jnp.int32, (L, 2 * LORA), 1)
        zt = jnp.where(lane < LORA, jnp.tanh(z), z).astype(BF16)
        lin = _dot(zt, wcomb_ref[...])
        yield
        w_log = -_softplus(-(w0_ref[...] + lin[:, :width])) - 0.5
        logw = -jnp.exp(w_log)
        alpha = _sigmoid(a0_ref[...] + lin[:, width:])

        kk = kb * kk_ref[...]
        kb = kb * (1.0 + (alpha - 1.0) * ka_ref[...])
        kk_sq, bonus_dot = head_sums(kk * kk, r * kb * rk_ref[...])
        yield
        kk = kk / jnp.maximum(jnp.sqrt(kk_sq), 1e-12)

        ti = lax.broadcasted_iota(jnp.int32, (L, L), 0)
        tj = lax.broadcasted_iota(jnp.int32, (L, L), 1)
        tri = (ti >= tj).astype(BF16)
        lw_hi = logw.astype(BF16)
        lw_lo = (logw - lw_hi.astype(F32)).astype(BF16)
        lp = _dot(tri, lw_hi) + _dot(tri, lw_lo)
        yield
        p_inv = jnp.exp(-lp)
        r_hat = r * jnp.exp(lp)
        a_hat = -kk * jnp.exp(lp - logw)
        b_til = kk * alpha * p_inv
        k_til = kb * p_inv
        out.update(
            ar=[jnp.concatenate([a_hat[:, s], r_hat[:, s]], axis=0).astype(BF16) for s in grp],
            bk=[jnp.concatenate([stack(b_til[:, s]), stack(k_til[:, s])], axis=0) for s in grp],
            vs=[stack(vb[:, s]) for s in grp],
            bkr=[jnp.concatenate([b_til[:, s], k_til[:, s]], axis=0).astype(BF16) for s in grp],
            vb=vb, p_last=jnp.exp(lp[L - 1:L, :]), bonus=bonus_dot * vb.astype(F32))

    row = lax.broadcasted_iota(jnp.int32, (L, gw), 0)
    col = lax.broadcasted_iota(jnp.int32, (L, gw), 1) % L
    strict = col < row
    incl = col <= row
    eye = (col == row).astype(F32)

    def solve(pre, s_old, out):
        ar, vs = pre["ar"], pre["vs"]
        gram = [_dot_nt(ar[g], pre["bk"][g]) for g in rng]
        yield
        a_ab = [jnp.where(strict, gm[:L, :gw], 0.0) for gm in gram]
        a_ak = [jnp.where(strict, gm[:L, gw:], 0.0).astype(BF16) for gm in gram]
        a_r = [jnp.concatenate([jnp.where(incl, gm[L:, :gw], 0.0),
                                jnp.where(incl, gm[L:, gw:], 0.0)], axis=1).astype(BF16)
               for gm in gram]

        t_inv = [eye + a for a in a_ab]
        pw = [_dot(a.astype(BF16), stack(a)) for a in a_ab]
        yield
        for _ in range(int(math.log2(L)) - 2):
            res = [_dot(jnp.concatenate([pw[g], t_inv[g]], axis=0).astype(BF16), stack(pw[g]))
                   for g in rng]
            yield
            t_inv = [t_inv[g] + res[g][L:] for g in rng]
            pw = [res[g][:L] for g in rng]
        t_inv = [(t_inv[g] + _dot(t_inv[g].astype(BF16), stack(pw[g]))).astype(BF16) for g in rng]
        yield

        asr = [_dot_nt(ar[g], s_old[g].astype(BF16)) for g in rng]
        yield
        x = [asr[g][:L] + _dot(a_ak[g], vs[g]) for g in rng]
        yield
        u = [_dot(t_inv[g], stack(x[g])) for g in rng]
        yield
        ys = [asr[g][L:] + _dot(a_r[g], jnp.concatenate([stack(u[g]), vs[g]], axis=0))
              for g in rng]
        yield
        s_new = []
        for g, s in enumerate(grp):
            upd = _dot_tn(jnp.concatenate([u[g].astype(BF16), pre["vb"][:, s]], axis=0),
                          pre["bkr"][g])
            s_new.append((s_old[g] + jnp.where(same_head, upd, 0.0)) * pre["p_last"][:, s])
        out.update(s_new=s_new, yb=jnp.concatenate(ys, axis=1))

    def finish(sub, pre, yb):
        inv_n = 1.0 / RW_HEAD
        mean = head_sums(yb)[0] * inv_n
        yield
        d = yb - mean
        var = head_sums(d * d)[0] * inv_n
        yield
        yn = d * lax.rsqrt(var + GN_EPS) * lnw_ref[...] + lnb_ref[...]
        rows = slice(sub * L, (sub + 1) * L)
        yb_ref[rows, :] = ((yn + pre["bonus"]) * gb_ref[rows, :].astype(F32)).astype(BF16)

    def run(main, side=None):
        for _ in main:
            if side is not None and next(side, "done") == "done":
                side = None
        for _ in side or ():
            pass

    state = [s_scr[g] for g in rng]
    pre, res = [dict() for _ in range(n_sub)], [dict() for _ in range(n_sub)]
    run(prepare(0, pre[0]))
    for sub in range(n_sub):
        side = prepare(sub + 1, pre[sub + 1]) if sub + 1 < n_sub else None
        if sub > 0:
            side = itertools.chain(finish(sub - 1, pre[sub - 1], res[sub - 1]["yb"]), side or ())
        run(solve(pre[sub], state, res[sub]), side)
        state = res[sub]["s_new"]
    run(finish(n_sub - 1, pre[n_sub - 1], res[n_sub - 1]["yb"]))
    for g in rng:
        s_scr[g] = state[g]

    @pl.when(i == pl.num_programs(1) - 1)
    def _():
        sout_ref[0] = s_scr[...]


def _rwkv(m2, gb2, s0g, wcomb, w0, a0, k_k, k_a, r_k, ln_w, ln_b, *, batch, seq):
    rows, shift_w = m2.shape
    width = gb2.shape[1]
    nc = seq // RW_CHUNK
    n_sub = next(n for n in (8, 4, 2, 1) if nc % n == 0)
    n_steps = nc // n_sub
    n_grp = width // MXU_N
    body = functools.partial(_rwkv_body, width=width, n_sub=n_sub)
    row_blk = lambda w: pl.BlockSpec((n_sub * RW_CHUNK, w), lambda b, i: (b * n_steps + i, 0))
    full = lambda a: pl.BlockSpec(a.shape, lambda b, i: (0,) * a.ndim)
    st = pl.BlockSpec((1, n_grp, MXU_N, MXU_N), lambda b, i: (b, 0, 0, 0))
    return pl.pallas_call(
        body,
        grid=(batch, n_steps),
        in_specs=[row_blk(shift_w), row_blk(width), st, full(wcomb), full(w0), full(a0),
                  full(k_k), full(k_a), full(r_k), full(ln_w), full(ln_b)],
        out_specs=[row_blk(width), st],
        out_shape=[jax.ShapeDtypeStruct((rows, width), BF16),
                   jax.ShapeDtypeStruct(s0g.shape, F32)],
        scratch_shapes=[pltpu.VMEM((n_grp, MXU_N, MXU_N), F32)],
        compiler_params=pltpu.CompilerParams(dimension_semantics=("arbitrary", "arbitrary"),
                                             vmem_limit_bytes=VMEM_LIMIT),
        name="rwkv7",
    )(m2, gb2, s0g, wcomb, w0, a0, k_k, k_a, r_k, ln_w, ln_b)


def _outproj_body(x_ref, ya_ref, yb_ref, wa_ref, wb_ref, npost_ref, y_ref):
    out = _dot(ya_ref[...], wa_ref[...]) + _dot(yb_ref[...], wb_ref[...])
    y = out * lax.rsqrt(jnp.mean(out * out, axis=-1, keepdims=True) + NORM_EPS)
    y_ref[...] = x_ref[...] + y * npost_ref[...]


def _outproj(x2, ya2, yb2, wa, wb, npost, *, tm):
    rows, d = x2.shape
    row_blk = lambda w: pl.BlockSpec((tm, w), lambda i: (i, 0))
    full = lambda a: pl.BlockSpec(a.shape, lambda i: (0,) * a.ndim)
    return pl.pallas_call(
        _outproj_body,
        grid=(rows // tm,),
        in_specs=[row_blk(d), row_blk(ya2.shape[1]), row_blk(yb2.shape[1]),
                  full(wa), full(wb), full(npost)],
        out_specs=row_blk(d),
        out_shape=jax.ShapeDtypeStruct((rows, d), F32),
        compiler_params=pltpu.CompilerParams(dimension_semantics=("arbitrary",),
                                             vmem_limit_bytes=VMEM_LIMIT),
        name="outproj",
    )(x2, ya2, yb2, wa, wb, npost)


def _rope_tables(pos):
    half = ROT_DIM // 2
    inv = jnp.power(jnp.float32(ROPE_THETA), -jnp.arange(half, dtype=F32) * (2.0 / ROT_DIM))
    ang = pos.astype(F32)[:, None] * inv[None, :]
    cos, sin = jnp.cos(ang), jnp.sin(ang)
    n = pos.shape[0]
    pad = HEAD_QK - ROT_DIM
    zeros, ones = jnp.zeros((n, half), F32), jnp.ones((n, pad), F32)
    zpad = jnp.zeros((n, pad), F32)
    c = jnp.concatenate([cos, cos, ones], axis=1)
    s_prev = jnp.concatenate([zeros, sin, zpad], axis=1)
    s_next = jnp.concatenate([-sin, zeros, zpad], axis=1)
    rep = LANES // HEAD_QK
    return tuple(jnp.tile(t, (1, rep)) for t in (c, s_prev, s_next))


def _group_states(s):
    b, h, n, _ = s.shape
    hpg = MXU_N // n
    s = s.reshape(b, h // hpg, hpg, n, n)
    out = jnp.einsum("bghij,hk->bghikj", s, jnp.eye(hpg, dtype=s.dtype))
    return out.reshape(b, h // hpg, MXU_N, MXU_N)


def _ungroup_states(sg):
    n = RW_HEAD
    hpg = MXU_N // n
    b, g = sg.shape[:2]
    s = jnp.einsum("bghikj,hk->bghij", sg.reshape(b, g, hpg, n, hpg, n), jnp.eye(hpg, dtype=sg.dtype))
    return s.reshape(b, g * hpg, n, n)


def _stream(x, pos, past_k, past_v, wkv0, shift0, p, lambda_init):
    batch, seq, d = x.shape
    width = p["w_out_a"].shape[0]
    heads = width // HEAD_V
    rows = batch * seq
    x2 = x.reshape(rows, d)
    tm_in, tm_out = min(256, seq), min(1024, seq)
    cos, s1, s2 = _rope_tables(pos)
    q2, k_out, kb2, v2, vt, ga2, m2, gb2, shift_out = _inproj(
        x2, shift0, p["norm_pre"], p["mu_shift"], cos, s1, s2, p["w_in"],
        batch=batch, seq=seq, tm=tm_in)

    if k_out.shape[0] == batch:
        k_out = k_out.reshape(batch, heads, 2, HEAD_QK, seq).transpose(0, 4, 1, 2, 3)
    else:
        k_out = k_out.reshape(batch, seq, heads, 2, HEAD_QK)
    k3 = kb2.reshape(batch, seq, width)
    v3 = v2.reshape(batch, seq, width)
    if past_k is None:
        k_all, v_all, q_off = k3, (vt if q2.ndim == 3 else v3), 0
        tq, tk = min(1024, seq), min(256, seq)
    else:
        past = past_k.shape[1]
        total = past + seq
        padded = -(-total // LANES) * LANES
        zpad = jnp.zeros((batch, padded - total, width), F32)
        k_all = jnp.concatenate([past_k.reshape(batch, past, width).astype(BF16), k3,
                                 zpad.astype(BF16)], axis=1)
        v_all = jnp.concatenate([past_v.reshape(batch, past, width), v3, zpad], axis=1)
        q_off, tq, tk = past, seq, padded
    ya2 = _attention(p["lam4"], p["subln"], q2, k_all, v_all, ga2, batch=batch, tq_len=seq,
                     tq=tq, tk=tk, q_off=q_off, lambda_init=lambda_init)

    yb2, s_groups = _rwkv(m2, gb2, _group_states(wkv0), p["w_comb"], p["w0"], p["a0"], p["k_k"],
                          p["k_a"], p["r_k"], p["ln_x_w"], p["ln_x_b"], batch=batch, seq=seq)

    y2 = _outproj(x2, ya2, yb2, p["w_out_a"], p["w_out_b"], p["norm_post"], tm=tm_out)
    return (y2.reshape(batch, seq, d),
            k_out,
            v3.reshape(batch, seq, heads, HEAD_V),
            _ungroup_states(s_groups),
            shift_out)


def kernel(x_prompt, x_sample, cache_k, cache_v, state_wkv, state_shift, norm_pre, w_in, lam_q1, lam_k1, lam_q2, lam_k2, subln, mu_shift, w0, w_up, a0, a_up, k_k, k_a, r_k, ln_x_w, ln_x_b, w_out, norm_post):
    depth = w_in.shape[0]
    assert depth == 1, "single-layer problem"
    l = 0
    lambda_init = 0.8 - 0.6 * math.exp(-0.3 * l)
    width = w_up.shape[2]
    row = lambda a: a[l].reshape(1, -1).astype(F32)
    zeros = jnp.zeros((LORA, width), F32)
    p = {
        "norm_pre": row(norm_pre), "norm_post": row(norm_post), "subln": row(subln),
        "mu_shift": row(mu_shift), "w0": row(w0), "a0": row(a0), "k_k": row(k_k), "k_a": row(k_a),
        "r_k": row(r_k), "ln_x_w": row(ln_x_w), "ln_x_b": row(ln_x_b),
        "w_in": w_in[l].astype(BF16),
        "w_out_a": w_out[l, :width].astype(BF16), "w_out_b": w_out[l, width:].astype(BF16),
        "w_comb": jnp.concatenate([jnp.concatenate([w_up[l], zeros], axis=1),
                                   jnp.concatenate([zeros, a_up[l]], axis=1)], axis=0).astype(BF16),
        "lam4": jnp.stack([lam_q1[l], lam_k1[l], lam_q2[l], lam_k2[l]]).astype(F32),
    }
    b_p, t_p, _ = x_prompt.shape
    b_s, t_s, _ = x_sample.shape
    past_len = cache_k.shape[2]
    shift_w = state_shift.shape[-1]
    n_heads_b = state_wkv.shape[2]
    pos_p = jnp.arange(t_p, dtype=jnp.int32)
    pos_s = past_len + jnp.arange(t_s, dtype=jnp.int32)

    yp, kp, vp, wp, sp = _stream(
        x_prompt, pos_p, None, None,
        jnp.zeros((b_p, n_heads_b, RW_HEAD, RW_HEAD), F32), jnp.zeros((b_p, 1, shift_w), F32),
        p, lambda_init)
    ys, ks, vs, ws, ss = _stream(
        x_sample, pos_s, cache_k[l], cache_v[l], state_wkv[l].astype(F32), state_shift[l],
        p, lambda_init)
    return (yp, ys, kp[None], vp[None], wp[None], sp[None],
            ks[None], vs[None], ws[None].astype(state_wkv.dtype), ss[None])
```
